```python
import jax, jax.numpy as jnp
from jax import lax
import numpy as np

D_MODEL = 2048
BATCH = 2
SEQ = 8192
DEPTH = 2
DEC_BATCH = 4
DEC_SEQ = 8192
PAST_LEN = 128

D_MIX = 3 * D_MODEL // 2
HEAD_DIM = 64
D_SSD = D_MIX // 2
SSD_HEADS = D_SSD // HEAD_DIM
SSD_GROUPS = 4
SSD_STATE = 128
SSD_CONV = 5
SSD_CHUNK = 128
XBC = D_SSD + 2 * SSD_GROUPS * SSD_STATE
D_SC = D_MIX // 4
SC_CONV = 3
D_ATT = D_MIX // 4
ATT_SLOTS = D_ATT // HEAD_DIM
DILATION_CFG = ((128, 1), (512, 4), (2048, 16))
N_DIL = len(DILATION_CFG)
ATT_HEADS = N_DIL * ATT_SLOTS
ATT_QB = 64
ROPE_DIM = HEAD_DIM // 4
ROPE_THETA = 500000.0
PEER_HEADS = 8
PEER_NKEYS = 128
PEER_EXPERTS = PEER_NKEYS * PEER_NKEYS
PEER_TOPK = 16
PEER_DQ = 256
PEER_CHUNK = 128
EPS = 1e-6
NEG = -1e30

D_IN = D_SSD + XBC + 2 * SSD_HEADS + 3 * D_SC + 3 * ATT_HEADS * HEAD_DIM
SPLITS = (D_SSD, D_SSD + XBC, D_SSD + XBC + 2 * SSD_HEADS, D_SSD + XBC + 2 * SSD_HEADS + 3 * D_SC)

kernel_name = 'hybrid_ssd_shortconv_dilattn_peer_encoder'


def rmsnorm(x, g):
    xf = x.astype(jnp.float32)
    y = xf * lax.rsqrt(jnp.mean(xf * xf, axis=-1, keepdims=True) + EPS) * g.astype(jnp.float32)
    return y.astype(x.dtype)


def dwconv(x, w, pad):
    c = x.shape[-1]
    return lax.conv_general_dilated(x, w[:, None, :].astype(x.dtype), window_strides=(1,),
                                    padding=[(pad, pad)], dimension_numbers=('NWC', 'WIO', 'NWC'),
                                    feature_group_count=c)


def rope_tables(length):
    inv = ROPE_THETA ** (-jnp.arange(0, ROPE_DIM, 2, dtype=jnp.float32) / ROPE_DIM)
    ang = jnp.arange(length, dtype=jnp.float32)[:, None] * inv[None, :]
    return jnp.cos(ang), jnp.sin(ang)


def apply_rope(t, cos, sin):
    tf = t.astype(jnp.float32)
    half = ROPE_DIM // 2
    c = cos[None, :, None, None, :]
    s = sin[None, :, None, None, :]
    x1 = tf[..., :half]
    x2 = tf[..., half:ROPE_DIM]
    out = jnp.concatenate([x1 * c - x2 * s, x2 * c + x1 * s, tf[..., ROPE_DIM:]], axis=-1)
    return out.astype(t.dtype)


def ssd_chunked(x, a, bm, cm):
    b, L, H, P = x.shape
    G, N = bm.shape[2], bm.shape[3]
    R = H // G
    Q = SSD_CHUNK
    c = L // Q
    x = x.reshape(b, c, Q, G, R, P)
    a = a.reshape(b, c, Q, G, R)
    bm = bm.reshape(b, c, Q, G, N)
    cm = cm.reshape(b, c, Q, G, N)
    acum = jnp.cumsum(a, axis=2)
    lower = jnp.tril(jnp.ones((Q, Q), dtype=bool))
    seg = acum[:, :, :, None] - acum[:, :, None, :]
    decay = jnp.exp(jnp.where(lower[None, None, :, :, None, None], seg, -jnp.inf))
    cb = jnp.einsum('bclgn,bcsgn->bclsg', cm, bm)
    y_diag = jnp.einsum('bclsgr,bcsgrp->bclgrp', cb[..., None] * decay, x)
    decay_end = jnp.exp(acum[:, :, -1:] - acum)
    states = jnp.einsum('bclgn,bclgrp->bcgrpn', bm, x * decay_end[..., None])
    chunk_decay = jnp.exp(acum[:, :, -1])

    def step(h, inp):
        dec, st = inp
        return dec[..., None, None] * h + st, h

    h0 = jnp.zeros((b, G, R, P, N), x.dtype)
    _, h_in = lax.scan(step, h0, (jnp.moveaxis(chunk_decay, 1, 0), jnp.moveaxis(states, 1, 0)))
    h_in = jnp.moveaxis(h_in, 0, 1)
    y_off = jnp.einsum('bclgn,bcgrpn->bclgrp', cm, h_in) * jnp.exp(acum)[..., None]
    return (y_diag + y_off).reshape(b, L, H, P)


def ssd_mixer(z, xbc, dt_raw, conv_w, conv_b, dt_bias, a_log, d_skip, norm_w):
    b, L, _ = z.shape
    xbc = jax.nn.silu(dwconv(xbc, conv_w, SSD_CONV // 2) + conv_b.astype(xbc.dtype))
    xs = xbc[..., :D_SSD].reshape(b, L, SSD_HEADS, HEAD_DIM).astype(jnp.float32)
    bm = xbc[..., D_SSD:D_SSD + SSD_GROUPS * SSD_STATE].reshape(b, L, SSD_GROUPS, SSD_STATE).astype(jnp.float32)
    cm = xbc[..., D_SSD + SSD_GROUPS * SSD_STATE:].reshape(b, L, SSD_GROUPS, SSD_STATE).astype(jnp.float32)
    dt = jax.nn.softplus(dt_raw.astype(jnp.float32).reshape(b, L, 2, SSD_HEADS) + dt_bias.astype(jnp.float32))
    A = -jnp.exp(a_log.astype(jnp.float32))
    y_f = ssd_chunked(xs * dt[:, :, 0, :, None], dt[:, :, 0] * A[0], bm, cm)
    fl = lambda t: jnp.flip(t, axis=1)
    y_b = fl(ssd_chunked(fl(xs * dt[:, :, 1, :, None]), fl(dt[:, :, 1] * A[1]), fl(bm), fl(cm)))
    y = y_f + y_b + xs * d_skip.astype(jnp.float32)[:, None]
    y = y.reshape(b, L, D_SSD) * jax.nn.silu(z.astype(jnp.float32))
    yg = y.reshape(b, L, SSD_GROUPS, D_SSD // SSD_GROUPS)
    yg = yg * lax.rsqrt(jnp.mean(yg * yg, axis=-1, keepdims=True) + EPS)
    return (yg.reshape(b, L, D_SSD) * norm_w.astype(jnp.float32)).astype(z.dtype)


def shortconv_mixer(sc, conv_w, norm_w):
    bg, cg, hx = jnp.split(sc, 3, axis=-1)
    y = bg * dwconv(cg * hx, conv_w, SC_CONV // 2)
    return rmsnorm(y, norm_w)


def dilated_band_attention(q, k, v, dil, hw):
    b, S, h, dh = q.shape
    L = S // dil

    def to_sub(t):
        return jnp.swapaxes(t.reshape(b, L, dil, h, dh), 1, 2).reshape(b * dil, L, h, dh)

    def from_sub(t):
        t = t.reshape((b, dil) + t.shape[1:])
        return jnp.swapaxes(t, 1, 2).reshape((b, S) + t.shape[3:])

    qs, ks, vs = to_sub(q), to_sub(k), to_sub(v)
    nb = -(-L // ATT_QB)
    Lp = nb * ATT_QB
    kb_len = ATT_QB + 2 * hw
    qs = jnp.pad(qs, ((0, 0), (0, Lp - L), (0, 0), (0, 0))).reshape(b * dil, nb, ATT_QB, h, dh)
    pad_kv = ((0, 0), (hw, Lp - L + hw), (0, 0), (0, 0))
    kp, vp = jnp.pad(ks, pad_kv), jnp.pad(vs, pad_kv)
    kidx = jnp.arange(nb)[:, None] * ATT_QB + jnp.arange(kb_len)[None, :]
    kblk, vblk = kp[:, kidx], vp[:, kidx]
    s = jnp.einsum('bnqhd,bnkhd->bnhqk', qs, kblk).astype(jnp.float32) * (dh ** -0.5)
    key_pos = kidx - hw
    q_pos = jnp.arange(nb)[:, None] * ATT_QB + jnp.arange(ATT_QB)[None, :]
    rel = key_pos[:, None, :] - q_pos[:, :, None]
    valid = (jnp.abs(rel) <= hw) & (key_pos[:, None, :] >= 0) & (key_pos[:, None, :] < L)
    s = jnp.where(valid[None, :, None], s, NEG)
    lse = jax.nn.logsumexp(s, axis=-1)
    p = jnp.exp(s - lse[..., None])
    o = jnp.einsum('bnhqk,bnkhd->bnqhd', p, vblk.astype(jnp.float32))
    o = o.reshape(b * dil, Lp, h, dh)[:, :L]
    lse = jnp.swapaxes(lse, 2, 3).reshape(b * dil, Lp, h)[:, :L]
    return from_sub(o), from_sub(lse)


def attention_mixer(att, norm_w, cos, sin):
    b, L, _ = att.shape
    att = att.reshape(b, L, 3, N_DIL, ATT_SLOTS, HEAD_DIM)
    q = apply_rope(att[:, :, 0], cos, sin)
    k = apply_rope(att[:, :, 1], cos, sin)
    v = att[:, :, 2]
    outs, lses = [], []
    for g, (win, dil) in enumerate(DILATION_CFG):
        o, lse = dilated_band_attention(q[:, :, g], k[:, :, g], v[:, :, g], dil, win // (2 * dil))
        outs.append(o)
        lses.append(lse)
    w = jax.nn.softmax(jnp.stack(lses, axis=0), axis=0)
    o = jnp.einsum('gbls,gblsd->blsd', w, jnp.stack(outs, axis=0))
    return rmsnorm(o.reshape(b, L, D_ATT).astype(att.dtype), norm_w)


def peer(h, wq, subkeys, u, v):
    b, L, D = h.shape
    t = h.reshape(b * L, D)
    T = t.shape[0]
    q = (t @ wq).reshape(T, PEER_HEADS, 2, PEER_DQ // 2).astype(jnp.float32)
    s = jnp.einsum('thid,hind->thin', q, subkeys.astype(jnp.float32))
    sv, si = lax.top_k(s, PEER_TOPK)
    cand = (sv[:, :, 0, :, None] + sv[:, :, 1, None, :]).reshape(T, PEER_HEADS, PEER_TOPK * PEER_TOPK)
    cidx = (si[:, :, 0, :, None] * PEER_NKEYS + si[:, :, 1, None, :]).reshape(T, PEER_HEADS, PEER_TOPK * PEER_TOPK)
    top, pos = lax.top_k(cand, PEER_TOPK)
    eidx = jnp.take_along_axis(cidx, pos, axis=-1)
    gate = jax.nn.softmax(top, axis=-1)
    nc = T // PEER_CHUNK

    def expert_block(args):
        tc, ec, gc = args
        act = jax.nn.gelu(jnp.einsum('thkd,td->thk', u[ec], tc).astype(jnp.float32), approximate=False) * gc
        return jnp.einsum('thk,thkd->td', act.astype(v.dtype), v[ec])

    out = lax.map(expert_block, (t.reshape(nc, PEER_CHUNK, D),
                                 eidx.reshape(nc, PEER_CHUNK, PEER_HEADS, PEER_TOPK),
                                 gate.reshape(nc, PEER_CHUNK, PEER_HEADS, PEER_TOPK)))
    return out.reshape(b, L, D).astype(h.dtype)


def trunk(x, norm_mix, w_in, ssd_conv_w, ssd_conv_b, ssd_dt_bias, ssd_a_log, ssd_d, ssd_norm,
          sc_conv_w, sc_norm, att_norm, w_out, norm_ffn, peer_wq, peer_subkeys, peer_u, peer_v, norm_final):
    cos, sin = rope_tables(x.shape[1])
    for l in range(DEPTH):
        h = rmsnorm(x, norm_mix[l])
        proj = h @ w_in[l]
        z, xbc, dt_raw, sc, att = jnp.split(proj, SPLITS, axis=-1)
        y = jnp.concatenate([
            ssd_mixer(z, xbc, dt_raw, ssd_conv_w[l], ssd_conv_b[l], ssd_dt_bias[l], ssd_a_log[l], ssd_d[l], ssd_norm[l]),
            shortconv_mixer(sc, sc_conv_w[l], sc_norm[l]),
            attention_mixer(att, att_norm[l], cos, sin)], axis=-1)
        x = x + y @ w_out[l]
        x = x + peer(rmsnorm(x, norm_ffn[l]), peer_wq[l], peer_subkeys[l], peer_u[l], peer_v[l])
    return rmsnorm(x, norm_final)


def setup_inputs(seed: int = 0) -> dict:
    key = jax.random.key(seed)
    ks = jax.random.split(key, 20)
    f32 = jnp.float32
    nrm = lambda k, shape: jax.random.normal(k, shape, f32)
    gain = lambda k, shape: 1.0 + 0.02 * nrm(k, shape)
    dt0 = jnp.exp(jax.random.uniform(ks[6], (DEPTH, 2, SSD_HEADS), f32, np.log(1e-3), np.log(1e-1)))
    return {
        'x_prompt': nrm(ks[0], (BATCH, SEQ, D_MODEL)),
        'x_sample': nrm(ks[1], (DEC_BATCH, DEC_SEQ, D_MODEL)),
        'norm_mix': gain(ks[2], (DEPTH, D_MODEL)),
        'w_in': nrm(ks[3], (DEPTH, D_MODEL, D_IN)) * D_MODEL ** -0.5,
        'ssd_conv_w': nrm(ks[4], (DEPTH, SSD_CONV, XBC)) * SSD_CONV ** -0.5,
        'ssd_conv_b': 0.02 * nrm(ks[5], (DEPTH, XBC)),
        'ssd_dt_bias': dt0 + jnp.log(-jnp.expm1(-dt0)),
        'ssd_a_log': jnp.log(jax.random.uniform(ks[7], (DEPTH, 2, SSD_HEADS), f32, 1.0, 16.0)),
        'ssd_d': gain(ks[8], (DEPTH, SSD_HEADS)),
        'ssd_norm': gain(ks[9], (DEPTH, D_SSD)),
        'sc_conv_w': nrm(ks[10], (DEPTH, SC_CONV, D_SC)) * SC_CONV ** -0.5,
        'sc_norm': gain(ks[11], (DEPTH, D_SC)),
        'att_norm': gain(ks[12], (DEPTH, D_ATT)),
        'w_out': nrm(ks[13], (DEPTH, D_MIX, D_MODEL)) * D_MIX ** -0.5,
        'norm_ffn': gain(ks[14], (DEPTH, D_MODEL)),
        'peer_wq': nrm(ks[15], (DEPTH, D_MODEL, PEER_HEADS * PEER_DQ)) * D_MODEL ** -0.5,
        'peer_subkeys': nrm(ks[16], (DEPTH, PEER_HEADS, 2, PEER_NKEYS, PEER_DQ // 2)) * (PEER_DQ // 2) ** -0.5,
        'peer_u': nrm(ks[17], (DEPTH, PEER_EXPERTS, D_MODEL)) * D_MODEL ** -0.5,
        'peer_v': nrm(ks[18], (DEPTH, PEER_EXPERTS, D_MODEL)) * (PEER_HEADS * PEER_TOPK) ** -0.5,
        'norm_final': gain(ks[19], (D_MODEL,)),
    }


def reference(x_prompt, x_sample, norm_mix, w_in, ssd_conv_w, ssd_conv_b, ssd_dt_bias, ssd_a_log, ssd_d,
              ssd_norm, sc_conv_w, sc_norm, att_norm, w_out, norm_ffn, peer_wq, peer_subkeys, peer_u, peer_v,
              norm_final):
    y_prompt = trunk(x_prompt, norm_mix, w_in, ssd_conv_w, ssd_conv_b, ssd_dt_bias, ssd_a_log, ssd_d, ssd_norm,
                     sc_conv_w, sc_norm, att_norm, w_out, norm_ffn, peer_wq, peer_subkeys, peer_u, peer_v, norm_final)
    y_sample = trunk(x_sample, norm_mix, w_in, ssd_conv_w, ssd_conv_b, ssd_dt_bias, ssd_a_log, ssd_d, ssd_norm,
                     sc_conv_w, sc_norm, att_norm, w_out, norm_ffn, peer_wq, peer_subkeys, peer_u, peer_v, norm_final)
    return (y_prompt, y_sample)
```

```python
import functools

import jax
import jax.numpy as jnp
import numpy as np
from jax import lax
from jax.experimental import pallas as pl
from jax.experimental.pallas import tpu as pltpu

D_MODEL = 2048
DEPTH = 2
D_MIX = 3 * D_MODEL // 2
HEAD_DIM = 64
D_SSD = D_MIX // 2
SSD_HEADS = D_SSD // HEAD_DIM
SSD_GROUPS = 4
SSD_STATE = 128
SSD_CONV = 5
SSD_CHUNK = 128
XBC = D_SSD + 2 * SSD_GROUPS * SSD_STATE
D_SC = D_MIX // 4
SC_CONV = 3
D_ATT = D_MIX // 4
ATT_SLOTS = D_ATT // HEAD_DIM
DILATION_CFG = ((128, 1), (512, 4), (2048, 16))
N_DIL = len(DILATION_CFG)
ATT_HEADS = N_DIL * ATT_SLOTS
ATT_QB = 64
ROPE_DIM = HEAD_DIM // 4
ROPE_THETA = 500000.0
PEER_HEADS = 8
PEER_NKEYS = 128
PEER_TOPK = 16
PEER_DQ = 256
PEER_CHUNK = 128
EPS = 1e-6
NEG = -1e30
DT_PAD = 128

VMEM_LIMIT_BYTES = 56 * 1024 * 1024

F32 = jnp.float32
BF16 = jnp.bfloat16


def _norm_matmul_kernel(x_ref, g_ref, w_ref, o_ref, h_scr):
    @pl.when(pl.program_id(1) == 0)
    def _():
        x = x_ref[...]
        ms = jnp.mean(x * x, axis=-1, keepdims=True)
        h_scr[...] = (x * lax.rsqrt(ms + EPS) * g_ref[...]).astype(BF16)

    o_ref[...] = jnp.dot(h_scr[...], w_ref[...], preferred_element_type=F32)


def norm_matmul(x, g, w, *, tm=512, tn=512):
    m, k = x.shape
    n = w.shape[1]
    tn = min(tn, n)
    assert m % tm == 0 and n % tn == 0
    return pl.pallas_call(
        _norm_matmul_kernel,
        grid=(m // tm, n // tn),
        in_specs=[
            pl.BlockSpec((tm, k), lambda i, j: (i, 0)),
            pl.BlockSpec((1, k), lambda i, j: (0, 0)),
            pl.BlockSpec((k, tn), lambda i, j: (0, j)),
        ],
        out_specs=pl.BlockSpec((tm, tn), lambda i, j: (i, j)),
        out_shape=jax.ShapeDtypeStruct((m, n), F32),
        scratch_shapes=[pltpu.VMEM((tm, k), BF16)],
        compiler_params=pltpu.CompilerParams(
            dimension_semantics=("parallel", "arbitrary"),
            vmem_limit_bytes=VMEM_LIMIT_BYTES),
        name="norm_matmul",
    )(x, g.reshape(1, k), w)


def _matmul_residual_kernel(y_ref, w_ref, r_ref, o_ref):
    o_ref[...] = r_ref[...] + jnp.dot(y_ref[...].astype(BF16), w_ref[...],
                                      preferred_element_type=F32)


def matmul_residual(y, w, res, *, tm=512, tn=512):
    m, k = y.shape
    n = w.shape[1]
    assert m % tm == 0 and n % tn == 0
    return pl.pallas_call(
        _matmul_residual_kernel,
        grid=(m // tm, n // tn),
        in_specs=[
            pl.BlockSpec((tm, k), lambda i, j: (i, 0)),
            pl.BlockSpec((k, tn), lambda i, j: (0, j)),
            pl.BlockSpec((tm, tn), lambda i, j: (i, j)),
        ],
        out_specs=pl.BlockSpec((tm, tn), lambda i, j: (i, j)),
        out_shape=jax.ShapeDtypeStruct((m, n), F32),
        compiler_params=pltpu.CompilerParams(
            dimension_semantics=("parallel", "arbitrary"),
            vmem_limit_bytes=VMEM_LIMIT_BYTES),
        name="matmul_residual",
    )(y, w, res)


def _rmsnorm_kernel(x_ref, g_ref, o_ref):
    x = x_ref[...]
    ms = jnp.mean(x * x, axis=-1, keepdims=True)
    o_ref[...] = x * lax.rsqrt(ms + EPS) * g_ref[...]


def rmsnorm_rows(x, g, *, tm=512):
    m, k = x.shape
    return pl.pallas_call(
        _rmsnorm_kernel,
        grid=(m // tm,),
        in_specs=[pl.BlockSpec((tm, k), lambda i: (i, 0)),
                  pl.BlockSpec((1, k), lambda i: (0, 0))],
        out_specs=pl.BlockSpec((tm, k), lambda i: (i, 0)),
        out_shape=jax.ShapeDtypeStruct((m, k), F32),
        compiler_params=pltpu.CompilerParams(dimension_semantics=("parallel",)),
        name="rmsnorm",
    )(x, g.reshape(1, k))


def rmsnorm(x, g):
    xf = x.astype(F32)
    y = xf * lax.rsqrt(jnp.mean(xf * xf, axis=-1, keepdims=True) + EPS) * g.astype(F32)
    return y.astype(x.dtype)


def dwconv(x, w, pad):
    c = x.shape[-1]
    return lax.conv_general_dilated(x, w[:, None, :].astype(x.dtype), window_strides=(1,),
                                    padding=[(pad, pad)], dimension_numbers=('NWC', 'WIO', 'NWC'),
                                    feature_group_count=c)


def rope_tables(length):
    inv = ROPE_THETA ** (-jnp.arange(0, ROPE_DIM, 2, dtype=F32) / ROPE_DIM)
    ang = jnp.arange(length, dtype=F32)[:, None] * inv[None, :]
    return jnp.cos(ang), jnp.sin(ang)


def apply_rope(t, cos, sin):
    tf = t.astype(F32)
    half = ROPE_DIM // 2
    c = cos[None, :, None, None, :]
    s = sin[None, :, None, None, :]
    x1 = tf[..., :half]
    x2 = tf[..., half:ROPE_DIM]
    out = jnp.concatenate([x1 * c - x2 * s, x2 * c + x1 * s, tf[..., ROPE_DIM:]], axis=-1)
    return out.astype(t.dtype)


def ssd_chunked(x, a, bm, cm):
    b, L, H, P = x.shape
    G, N = bm.shape[2], bm.shape[3]
    R = H // G
    Q = SSD_CHUNK
    c = L // Q
    x = x.reshape(b, c, Q, G, R, P)
    a = a.reshape(b, c, Q, G, R)
    bm = bm.reshape(b, c, Q, G, N)
    cm = cm.reshape(b, c, Q, G, N)
    acum = jnp.cumsum(a, axis=2)
    lower = jnp.tril(jnp.ones((Q, Q), dtype=bool))
    seg = acum[:, :, :, None] - acum[:, :, None, :]
    decay = jnp.exp(jnp.where(lower[None, None, :, :, None, None], seg, -jnp.inf))
    cb = jnp.einsum('bclgn,bcsgn->bclsg', cm, bm)
    y_diag = jnp.einsum('bclsgr,bcsgrp->bclgrp', cb[..., None] * decay, x)
    decay_end = jnp.exp(acum[:, :, -1:] - acum)
    states = jnp.einsum('bclgn,bclgrp->bcgrpn', bm, x * decay_end[..., None])
    chunk_decay = jnp.exp(acum[:, :, -1])

    def step(h, inp):
        dec, st = inp
        return dec[..., None, None] * h + st, h

    h0 = jnp.zeros((b, G, R, P, N), x.dtype)
    _, h_in = lax.scan(step, h0, (jnp.moveaxis(chunk_decay, 1, 0), jnp.moveaxis(states, 1, 0)))
    h_in = jnp.moveaxis(h_in, 0, 1)
    y_off = jnp.einsum('bclgn,bcgrpn->bclgrp', cm, h_in) * jnp.exp(acum)[..., None]
    return (y_diag + y_off).reshape(b, L, H, P)


def ssd_mixer(z, xbc, dt_raw, conv_w, conv_b, dt_bias, a_log, d_skip, norm_w):
    b, L, _ = z.shape
    xbc = jax.nn.silu(dwconv(xbc, conv_w, SSD_CONV // 2) + conv_b.astype(xbc.dtype))
    xs = xbc[..., :D_SSD].reshape(b, L, SSD_HEADS, HEAD_DIM).astype(F32)
    bm = xbc[..., D_SSD:D_SSD + SSD_GROUPS * SSD_STATE].reshape(b, L, SSD_GROUPS, SSD_STATE).astype(F32)
    cm = xbc[..., D_SSD + SSD_GROUPS * SSD_STATE:].reshape(b, L, SSD_GROUPS, SSD_STATE).astype(F32)
    dt = jax.nn.softplus(dt_raw.astype(F32).reshape(b, L, 2, SSD_HEADS) + dt_bias.astype(F32))
    A = -jnp.exp(a_log.astype(F32))
    y_f = ssd_chunked(xs * dt[:, :, 0, :, None], dt[:, :, 0] * A[0], bm, cm)
    fl = lambda t: jnp.flip(t, axis=1)
    y_b = fl(ssd_chunked(fl(xs * dt[:, :, 1, :, None]), fl(dt[:, :, 1] * A[1]), fl(bm), fl(cm)))
    y = y_f + y_b + xs * d_skip.astype(F32)[:, None]
    y = y.reshape(b, L, D_SSD) * jax.nn.silu(z.astype(F32))
    yg = y.reshape(b, L, SSD_GROUPS, D_SSD // SSD_GROUPS)
    yg = yg * lax.rsqrt(jnp.mean(yg * yg, axis=-1, keepdims=True) + EPS)
    return (yg.reshape(b, L, D_SSD) * norm_w.astype(F32)).astype(z.dtype)


def shortconv_mixer(sc, conv_w, norm_w):
    bg, cg, hx = jnp.split(sc, 3, axis=-1)
    y = bg * dwconv(cg * hx, conv_w, SC_CONV // 2)
    return rmsnorm(y, norm_w)


def dilated_band_attention(q, k, v, dil, hw):
    b, S, h, dh = q.shape
    L = S // dil

    def to_sub(t):
        return jnp.swapaxes(t.reshape(b, L, dil, h, dh), 1, 2).reshape(b * dil, L, h, dh)

    def from_sub(t):
        t = t.reshape((b, dil) + t.shape[1:])
        return jnp.swapaxes(t, 1, 2).reshape((b, S) + t.shape[3:])

    qs, ks, vs = to_sub(q), to_sub(k), to_sub(v)
    nb = -(-L // ATT_QB)
    Lp = nb * ATT_QB
    kb_len = ATT_QB + 2 * hw
    qs = jnp.pad(qs, ((0, 0), (0, Lp - L), (0, 0), (0, 0))).reshape(b * dil, nb, ATT_QB, h, dh)
    pad_kv = ((0, 0), (hw, Lp - L + hw), (0, 0), (0, 0))
    kp, vp = jnp.pad(ks, pad_kv), jnp.pad(vs, pad_kv)
    kidx = jnp.arange(nb)[:, None] * ATT_QB + jnp.arange(kb_len)[None, :]
    kblk, vblk = kp[:, kidx], vp[:, kidx]
    s = jnp.einsum('bnqhd,bnkhd->bnhqk', qs, kblk).astype(F32) * (dh ** -0.5)
    key_pos = kidx - hw
    q_pos = jnp.arange(nb)[:, None] * ATT_QB + jnp.arange(ATT_QB)[None, :]
    rel = key_pos[:, None, :] - q_pos[:, :, None]
    valid = (jnp.abs(rel) <= hw) & (key_pos[:, None, :] >= 0) & (key_pos[:, None, :] < L)
    s = jnp.where(valid[None, :, None], s, NEG)
    lse = jax.nn.logsumexp(s, axis=-1)
    p = jnp.exp(s - lse[..., None])
    o = jnp.einsum('bnhqk,bnkhd->bnqhd', p, vblk.astype(F32))
    o = o.reshape(b * dil, Lp, h, dh)[:, :L]
    lse = jnp.swapaxes(lse, 2, 3).reshape(b * dil, Lp, h)[:, :L]
    return from_sub(o), from_sub(lse)


def attention_mixer(att, norm_w, cos, sin):
    b, L, _ = att.shape
    att = att.reshape(b, L, 3, N_DIL, ATT_SLOTS, HEAD_DIM)
    q = apply_rope(att[:, :, 0], cos, sin)
    k = apply_rope(att[:, :, 1], cos, sin)
    v = att[:, :, 2]
    outs, lses = [], []
    for g, (win, dil) in enumerate(DILATION_CFG):
        o, lse = dilated_band_attention(q[:, :, g], k[:, :, g], v[:, :, g], dil, win // (2 * dil))
        outs.append(o)
        lses.append(lse)
    w = jax.nn.softmax(jnp.stack(lses, axis=0), axis=0)
    o = jnp.einsum('gbls,gblsd->blsd', w, jnp.stack(outs, axis=0))
    return rmsnorm(o.reshape(b, L, D_ATT).astype(att.dtype), norm_w)


def peer_from_queries(t, q, subkeys, u, v):
    T, D = t.shape
    q = q.reshape(T, PEER_HEADS, 2, PEER_DQ // 2).astype(F32)
    s = jnp.einsum('thid,hind->thin', q, subkeys.astype(F32))
    sv, si = lax.top_k(s, PEER_TOPK)
    cand = (sv[:, :, 0, :, None] + sv[:, :, 1, None, :]).reshape(T, PEER_HEADS, PEER_TOPK * PEER_TOPK)
    cidx = (si[:, :, 0, :, None] * PEER_NKEYS + si[:, :, 1, None, :]).reshape(T, PEER_HEADS, PEER_TOPK * PEER_TOPK)
    top, pos = lax.top_k(cand, PEER_TOPK)
    eidx = jnp.take_along_axis(cidx, pos, axis=-1)
    gate = jax.nn.softmax(top, axis=-1)
    nc = T // PEER_CHUNK

    def expert_block(args):
        tc, ec, gc = args
        act = jax.nn.gelu(jnp.einsum('thkd,td->thk', u[ec], tc).astype(F32), approximate=False) * gc
        return jnp.einsum('thk,thkd->td', act.astype(v.dtype), v[ec])

    out = lax.map(expert_block, (t.reshape(nc, PEER_CHUNK, D),
                                 eidx.reshape(nc, PEER_CHUNK, PEER_HEADS, PEER_TOPK),
                                 gate.reshape(nc, PEER_CHUNK, PEER_HEADS, PEER_TOPK)))
    return out


def trunk(x, norm_mix, w_in, ssd_conv_w, ssd_conv_b, ssd_dt_bias, ssd_a_log, ssd_d, ssd_norm,
          sc_conv_w, sc_norm, att_norm, w_out, norm_ffn, peer_wq, peer_subkeys, peer_u, peer_v, norm_final):
    b, L, D = x.shape
    T = b * L
    cos, sin = rope_tables(L)
    x = x.reshape(T, D)
    o_xbc = D_SSD
    o_dt = o_xbc + XBC
    o_sc = o_dt + 2 * SSD_HEADS
    o_att = o_sc + 3 * D_SC
    for l in range(DEPTH):
        w = w_in[l]
        g = norm_mix[l]
        w_dt = jnp.pad(w[:, o_dt:o_sc], ((0, 0), (0, DT_PAD - 2 * SSD_HEADS)))
        z = norm_matmul(x, g, w[:, :o_xbc].astype(BF16))
        xbc = norm_matmul(x, g, w[:, o_xbc:o_dt].astype(BF16))
        dt_raw = norm_matmul(x, g, w_dt.astype(BF16))[:, :2 * SSD_HEADS]
        sc = norm_matmul(x, g, w[:, o_sc:o_att].astype(BF16), tn=768)
        att = norm_matmul(x, g, w[:, o_att:].astype(BF16), tn=768)
        r3 = lambda t: t.reshape(b, L, t.shape[-1])
        y = jnp.concatenate([
            ssd_mixer(r3(z), r3(xbc), r3(dt_raw), ssd_conv_w[l], ssd_conv_b[l], ssd_dt_bias[l],
                      ssd_a_log[l], ssd_d[l], ssd_norm[l]),
            shortconv_mixer(r3(sc), sc_conv_w[l], sc_norm[l]),
            attention_mixer(r3(att), att_norm[l], cos, sin)], axis=-1)
        x = matmul_residual(y.reshape(T, D_MIX), w_out[l].astype(BF16), x)
        q = norm_matmul(x, norm_ffn[l], peer_wq[l].astype(BF16))
        h = rmsnorm_rows(x, norm_ffn[l])
        x = x + peer_from_queries(h, q, peer_subkeys[l], peer_u[l], peer_v[l]).reshape(T, D)
    return rmsnorm_rows(x, norm_final).reshape(b, L, D)


def kernel(x_prompt, x_sample, norm_mix, w_in, ssd_conv_w, ssd_conv_b, ssd_dt_bias, ssd_a_log, ssd_d, ssd_norm,
           sc_conv_w, sc_norm, att_norm, w_out, norm_ffn, peer_wq, peer_subkeys, peer_u, peer_v, norm_final):
    nb = x_prompt.shape[0]
    x = jnp.concatenate([x_prompt, x_sample], axis=0)
    y = trunk(x, norm_mix, w_in, ssd_conv_w, ssd_conv_b, ssd_dt_bias, ssd_a_log, ssd_d, ssd_norm,
              sc_conv_w, sc_norm, att_norm, w_out, norm_ffn, peer_wq, peer_subkeys, peer_u, peer_v, norm_final)
    return (y[:nb], y[nb:])
```

```python
import functools

import jax
import jax.numpy as jnp
import numpy as np
from jax import lax
from jax.experimental import pallas as pl
from jax.experimental.pallas import tpu as pltpu

D_MODEL = 2048
DEPTH = 2
D_MIX = 3 * D_MODEL // 2
HEAD_DIM = 64
D_SSD = D_MIX // 2
SSD_HEADS = D_SSD // HEAD_DIM
SSD_GROUPS = 4
SSD_STATE = 128
SSD_CONV = 5
SSD_CHUNK = 128
XBC = D_SSD + 2 * SSD_GROUPS * SSD_STATE
D_SC = D_MIX // 4
SC_CONV = 3
D_ATT = D_MIX // 4
ATT_SLOTS = D_ATT // HEAD_DIM
DILATION_CFG = ((128, 1), (512, 4), (2048, 16))
N_DIL = len(DILATION_CFG)
ATT_HEADS = N_DIL * ATT_SLOTS
ATT_QB = 64
ROPE_DIM = HEAD_DIM // 4
ROPE_THETA = 500000.0
PEER_HEADS = 8
PEER_NKEYS = 128
PEER_TOPK = 16
PEER_DQ = 256
EPS = 1e-6
NEG = -1e30
DT_PAD = 128

LANES = 128
SUBLANES = 8
D_ROWS = D_MODEL // LANES
P_ROWS = D_ROWS // 2
PEER_TILE = 4096
PEER_BLK = 128
PEER_CHUNK_SLOTS = 256
GROUP_UNROLL = 4

VMEM_LIMIT_BYTES = 56 * 1024 * 1024
PEER_VMEM_LIMIT = VMEM_LIMIT_BYTES

F32 = jnp.float32
BF16 = jnp.bfloat16


def _norm_matmul_kernel(x_ref, g_ref, w_ref, o_ref, h_scr):
    @pl.when(pl.program_id(1) == 0)
    def _():
        x = x_ref[...]
        ms = jnp.mean(x * x, axis=-1, keepdims=True)
        h_scr[...] = (x * lax.rsqrt(ms + EPS) * g_ref[...]).astype(BF16)

    o_ref[...] = jnp.dot(h_scr[...], w_ref[...], preferred_element_type=F32)


def norm_matmul(x, g, w, *, tm=512, tn=512):
    m, k = x.shape
    n = w.shape[1]
    tn = min(tn, n)
    assert m % tm == 0 and n % tn == 0
    return pl.pallas_call(
        _norm_matmul_kernel,
        grid=(m // tm, n // tn),
        in_specs=[
            pl.BlockSpec((tm, k), lambda i, j: (i, 0)),
            pl.BlockSpec((1, k), lambda i, j: (0, 0)),
            pl.BlockSpec((k, tn), lambda i, j: (0, j)),
        ],
        out_specs=pl.BlockSpec((tm, tn), lambda i, j: (i, j)),
        out_shape=jax.ShapeDtypeStruct((m, n), F32),
        scratch_shapes=[pltpu.VMEM((tm, k), BF16)],
        compiler_params=pltpu.CompilerParams(
            dimension_semantics=("parallel", "arbitrary"),
            vmem_limit_bytes=VMEM_LIMIT_BYTES),
        name="norm_matmul",
    )(x, g.reshape(1, k), w)


def _matmul_residual_kernel(y_ref, w_ref, r_ref, o_ref):
    o_ref[...] = r_ref[...] + jnp.dot(y_ref[...].astype(BF16), w_ref[...],
                                      preferred_element_type=F32)


def matmul_residual(y, w, res, *, tm=512, tn=512):
    m, k = y.shape
    n = w.shape[1]
    assert m % tm == 0 and n % tn == 0
    return pl.pallas_call(
        _matmul_residual_kernel,
        grid=(m // tm, n // tn),
        in_specs=[
            pl.BlockSpec((tm, k), lambda i, j: (i, 0)),
            pl.BlockSpec((k, tn), lambda i, j: (0, j)),
            pl.BlockSpec((tm, tn), lambda i, j: (i, j)),
        ],
        out_specs=pl.BlockSpec((tm, tn), lambda i, j: (i, j)),
        out_shape=jax.ShapeDtypeStruct((m, n), F32),
        compiler_params=pltpu.CompilerParams(
            dimension_semantics=("parallel", "arbitrary"),
            vmem_limit_bytes=VMEM_LIMIT_BYTES),
        name="matmul_residual",
    )(y, w, res)


def _rmsnorm_kernel(x_ref, g_ref, o_ref):
    x = x_ref[...]
    ms = jnp.mean(x * x, axis=-1, keepdims=True)
    o_ref[...] = x * lax.rsqrt(ms + EPS) * g_ref[...]


def rmsnorm_rows(x, g, *, tm=512):
    m, k = x.shape
    return pl.pallas_call(
        _rmsnorm_kernel,
        grid=(m // tm,),
        in_specs=[pl.BlockSpec((tm, k), lambda i: (i, 0)),
                  pl.BlockSpec((1, k), lambda i: (0, 0))],
        out_specs=pl.BlockSpec((tm, k), lambda i: (i, 0)),
        out_shape=jax.ShapeDtypeStruct((m, k), F32),
        compiler_params=pltpu.CompilerParams(dimension_semantics=("parallel",)),
        name="rmsnorm",
    )(x, g.reshape(1, k))


def _fold(xa, xb, mask, shift):
    s = jnp.where(mask, xa, xb)
    t = jnp.where(mask, xb, xa)
    return s + pltpu.roll(t, shift, axis=0)


def _unpack(w):
    lo = lax.bitcast_convert_type(w << 16, F32)
    hi = lax.bitcast_convert_type(w & jnp.int32(-65536), F32)
    return lo, hi


def _split3(x):
    p0 = x.astype(BF16)
    r = x - p0.astype(F32)
    p1 = r.astype(BF16)
    p2 = (r - p1.astype(F32)).astype(BF16)
    return p0, p1, p2


def _peer_dot_kernel(meta_ref, *refs, chunk):
    row_refs = refs[:SUBLANES]
    gtok_ref, x_ref, gate_ref, u_ref, a_ref, c_scr = refs[SUBLANES:]
    k = pl.program_id(1)
    gpc = chunk // SUBLANES

    @pl.when(k == 0)
    def _():
        a_ref[...] = jnp.zeros_like(a_ref)

    row_i = lax.broadcasted_iota(jnp.int32, (SUBLANES, LANES), 0)
    m1 = (row_i & 1) == 0
    m2 = (row_i & 2) == 0
    m3 = (row_i & 4) == 0
    ones = jnp.ones((SUBLANES, LANES), BF16)

    def chunk_body(c, carry):
        def groups(gi, carry):
            for sub in range(GROUP_UNROLL):
                gl = gi * GROUP_UNROLL + sub
                g = c * gpc + gl
                tok = gtok_ref[0, 0, g]
                xlo = x_ref[tok, 0:P_ROWS, :]
                xhi = x_ref[tok, P_ROWS:D_ROWS, :]
                ps = []
                for i in range(SUBLANES):
                    r8 = pl.multiple_of(row_refs[i][0, 0, g], SUBLANES)
                    lo, hi = _unpack(u_ref[pl.ds(r8, SUBLANES), :])
                    ps.append(lo * xlo + hi * xhi)
                q = [_fold(ps[2 * i], ps[2 * i + 1], m1, 1) for i in range(4)]
                r = [_fold(q[0], q[1], m2, 2), _fold(q[2], q[3], m2, 2)]
                c_scr[pl.ds(pl.multiple_of(gl * SUBLANES, SUBLANES), SUBLANES), :] = _fold(r[0], r[1], m3, 4)
            return carry

        lax.fori_loop(0, gpc // GROUP_UNROLL, groups, 0)
        dims = (((1,), (1,)), ((), ()))
        h8 = sum(lax.dot_general(ones, p, dims, preferred_element_type=F32) for p in _split3(c_scr[...]))
        h = h8[0:1, :]
        off = pl.multiple_of(c * chunk, chunk)
        act = 0.5 * h * (1.0 + lax.erf(h * 0.7071067811865476))
        a_ref[0, :, pl.ds(off, chunk)] = act * gate_ref[0, :, pl.ds(off, chunk)]
        return carry

    lax.fori_loop(meta_ref[0, 0, k], meta_ref[0, 0, k + 1], chunk_body, 0)


def _peer_acc_kernel(meta_ref, *refs, chunk):
    row_refs = refs[:SUBLANES]
    a_refs = refs[SUBLANES:2 * SUBLANES]
    gtok_ref, res_ref, v_ref, o_ref = refs[2 * SUBLANES:]
    k = pl.program_id(1)
    gpc = chunk // SUBLANES

    @pl.when(k == 0)
    def _():
        o_ref[...] = res_ref[...]

    def groups(gp, carry):
        for sub in range(GROUP_UNROLL):
            g = gp * GROUP_UNROLL + sub
            tok = gtok_ref[0, 0, g]
            alo = ahi = None
            for i in range(SUBLANES):
                r8 = pl.multiple_of(row_refs[i][0, 0, g], SUBLANES)
                lo, hi = _unpack(v_ref[pl.ds(r8, SUBLANES), :])
                a = a_refs[i][0, 0, g]
                alo = a * lo if alo is None else alo + a * lo
                ahi = a * hi if ahi is None else ahi + a * hi
            o_ref[tok, 0:P_ROWS, :] += alo
            o_ref[tok, P_ROWS:D_ROWS, :] += ahi
        return carry

    per_chunk = gpc // GROUP_UNROLL
    lax.fori_loop(meta_ref[0, 0, k] * per_chunk, meta_ref[0, 0, k + 1] * per_chunk, groups, 0)


def pack_table(w):
    e = w.shape[0]
    bits = lax.bitcast_convert_type(w.astype(BF16), jnp.uint16).astype(jnp.uint32)
    half = D_MODEL // 2
    packed = (bits[:, half:] << 16) | bits[:, :half]
    return lax.bitcast_convert_type(packed, jnp.int32).reshape(e * P_ROWS, LANES)


def by_lane(a):
    nb, cap = a.shape[0], a.shape[-1]
    a = a.reshape(nb, cap // SUBLANES, SUBLANES)
    return [a[:, :, i].reshape(nb, 1, cap // SUBLANES) for i in range(SUBLANES)]


def peer_route(eidx, gate, *, n_exp, tile, blk, chunk):
    T, S = eidx.shape
    nb = T // blk
    nt = n_exp // tile
    shift = tile.bit_length() - 1
    assert 1 << shift == tile and T % blk == 0 and blk * 256 < (1 << 20) and S + SUBLANES <= 256 and chunk <= 256
    i32 = jnp.int32
    tiles = jnp.arange(nt, dtype=i32)
    tile_id = eidx >> shift
    cnt = jnp.sum(tile_id[:, :, None] == tiles[None, None, :], axis=1, dtype=i32)
    tpad = (-cnt) % SUBLANES
    t_in = (jnp.arange(T, dtype=i32) % blk)[:, None]
    dead = i32(nt << 20)
    key_real = (tile_id << 20) | (t_in << 8) | jnp.arange(S, dtype=i32)[None, :]
    i8 = jnp.arange(SUBLANES, dtype=i32)
    key_tpad = jnp.where(i8[None, None, :] < tpad[:, :, None],
                         (tiles[None, :, None] << 20) | (t_in[:, :, None] << 8) | (S + i8)[None, None, :], dead)
    n_run = jnp.sum((cnt + tpad).reshape(nb, blk, nt), axis=1)
    bpad = (-n_run) % chunk
    ib = jnp.arange(chunk - SUBLANES, dtype=i32)
    key_bpad = jnp.where(ib[None, None, :] < bpad[:, :, None],
                         (tiles[None, :, None] << 20) | i32(blk << 8) | ib[None, None, :], dead)
    n_real, n_tp, n_bp = blk * S, blk * nt * SUBLANES, nt * (chunk - SUBLANES)
    cap = -(-(n_real + n_tp + nt * chunk) // 1024) * 1024
    fill = cap - (n_real + n_tp + n_bp)
    keys = jnp.concatenate([key_real.reshape(nb, n_real), key_tpad.reshape(nb, n_tp), key_bpad.reshape(nb, n_bp),
                            jnp.full((nb, fill), dead, i32)], axis=1)
    zpad = lambda a: jnp.concatenate([a.reshape(nb, n_real), jnp.zeros((nb, cap - n_real), a.dtype)], axis=1)
    rows = zpad((eidx & (tile - 1)) * SUBLANES)
    keys, rows, gates = lax.sort((keys, rows, zpad(gate)), dimension=1, num_keys=1)
    gtok = (keys[:, ::SUBLANES] >> 8) & (blk - 1)
    n_chunks = (n_run + bpad) // chunk
    cstart = jnp.concatenate([jnp.zeros((nb, 1), i32), jnp.cumsum(n_chunks, axis=1)], axis=1)
    meta = jnp.pad(cstart, ((0, 0), (0, LANES - nt - 1)))
    r3 = lambda a: a.reshape(nb, 1, a.shape[1])
    return r3(meta), by_lane(rows), r3(gtok), r3(gates), cap


def peer_experts(h, res, eidx, gate, u_packed, v_packed, *, tile, blk):
    T = h.shape[0]
    chunk = PEER_CHUNK_SLOTS
    n_exp = u_packed.shape[0] // P_ROWS
    nt = n_exp // tile
    nb = T // blk
    meta, rows, gtok, gates, cap = peer_route(eidx, gate, n_exp=n_exp, tile=tile, blk=blk, chunk=chunk)
    tok_spec = pl.BlockSpec((blk, D_ROWS, LANES), lambda j, k: (j, 0, 0))
    tab_spec = pl.BlockSpec((tile * P_ROWS, LANES), lambda j, k: (k, 0))
    smem = lambda n: pl.BlockSpec((1, 1, n), lambda j, k: (j, 0, 0), memory_space=pltpu.SMEM,
                                  pipeline_mode=pl.Buffered(1))
    slot_spec = pl.BlockSpec((1, 1, cap), lambda j, k: (j, 0, 0))
    params = pltpu.CompilerParams(dimension_semantics=("parallel", "arbitrary"), vmem_limit_bytes=PEER_VMEM_LIMIT)
    a = pl.pallas_call(
        functools.partial(_peer_dot_kernel, chunk=chunk),
        grid=(nb, nt),
        in_specs=[smem(LANES)] + [smem(cap // SUBLANES)] * (SUBLANES + 1) + [tok_spec, slot_spec, tab_spec],
        out_specs=slot_spec,
        out_shape=jax.ShapeDtypeStruct((nb, 1, cap), F32),
        scratch_shapes=[pltpu.VMEM((chunk, LANES), F32)],
        compiler_params=params, name="peer_dot",
    )(meta, *rows, gtok, h.reshape(T, D_ROWS, LANES), gates, u_packed)
    out = pl.pallas_call(
        functools.partial(_peer_acc_kernel, chunk=chunk),
        grid=(nb, nt),
        in_specs=[smem(LANES)] + [smem(cap // SUBLANES)] * (2 * SUBLANES + 1) + [tok_spec, tab_spec],
        out_specs=tok_spec,
        out_shape=jax.ShapeDtypeStruct((T, D_ROWS, LANES), F32),
        compiler_params=params, name="peer_acc",
    )(meta, *rows, *by_lane(a.reshape(nb, cap)), gtok, res.reshape(T, D_ROWS, LANES), v_packed)
    return out.reshape(T, D_MODEL)


def rmsnorm(x, g):
    xf = x.astype(F32)
    y = xf * lax.rsqrt(jnp.mean(xf * xf, axis=-1, keepdims=True) + EPS) * g.astype(F32)
    return y.astype(x.dtype)


def dwconv(x, w, pad):
    c = x.shape[-1]
    return lax.conv_general_dilated(x, w[:, None, :].astype(x.dtype), window_strides=(1,),
                                    padding=[(pad, pad)], dimension_numbers=('NWC', 'WIO', 'NWC'),
                                    feature_group_count=c)


def rope_tables(length):
    inv = ROPE_THETA ** (-jnp.arange(0, ROPE_DIM, 2, dtype=F32) / ROPE_DIM)
    ang = jnp.arange(length, dtype=F32)[:, None] * inv[None, :]
    return jnp.cos(ang), jnp.sin(ang)


def apply_rope(t, cos, sin):
    tf = t.astype(F32)
    half = ROPE_DIM // 2
    c = cos[None, :, None, None, :]
    s = sin[None, :, None, None, :]
    x1 = tf[..., :half]
    x2 = tf[..., half:ROPE_DIM]
    out = jnp.concatenate([x1 * c - x2 * s, x2 * c + x1 * s, tf[..., ROPE_DIM:]], axis=-1)
    return out.astype(t.dtype)


def ssd_chunked(x, a, bm, cm):
    b, L, H, P = x.shape
    G, N = bm.shape[2], bm.shape[3]
    R = H // G
    Q = SSD_CHUNK
    c = L // Q
    x = x.reshape(b, c, Q, G, R, P)
    a = a.reshape(b, c, Q, G, R)
    bm = bm.reshape(b, c, Q, G, N)
    cm = cm.reshape(b, c, Q, G, N)
    acum = jnp.cumsum(a, axis=2)
    lower = jnp.tril(jnp.ones((Q, Q), dtype=bool))
    seg = acum[:, :, :, None] - acum[:, :, None, :]
    decay = jnp.exp(jnp.where(lower[None, None, :, :, None, None], seg, -jnp.inf))
    cb = jnp.einsum('bclgn,bcsgn->bclsg', cm, bm)
    y_diag = jnp.einsum('bclsgr,bcsgrp->bclgrp', cb[..., None] * decay, x)
    decay_end = jnp.exp(acum[:, :, -1:] - acum)
    states = jnp.einsum('bclgn,bclgrp->bcgrpn', bm, x * decay_end[..., None])
    chunk_decay = jnp.exp(acum[:, :, -1])

    def step(h, inp):
        dec, st = inp
        return dec[..., None, None] * h + st, h

    h0 = jnp.zeros((b, G, R, P, N), x.dtype)
    _, h_in = lax.scan(step, h0, (jnp.moveaxis(chunk_decay, 1, 0), jnp.moveaxis(states, 1, 0)))
    h_in = jnp.moveaxis(h_in, 0, 1)
    y_off = jnp.einsum('bclgn,bcgrpn->bclgrp', cm, h_in) * jnp.exp(acum)[..., None]
    return (y_diag + y_off).reshape(b, L, H, P)


def ssd_mixer(z, xbc, dt_raw, conv_w, conv_b, dt_bias, a_log, d_skip, norm_w):
    b, L, _ = z.shape
    xbc = jax.nn.silu(dwconv(xbc, conv_w, SSD_CONV // 2) + conv_b.astype(xbc.dtype))
    xs = xbc[..., :D_SSD].reshape(b, L, SSD_HEADS, HEAD_DIM).astype(F32)
    bm = xbc[..., D_SSD:D_SSD + SSD_GROUPS * SSD_STATE].reshape(b, L, SSD_GROUPS, SSD_STATE).astype(F32)
    cm = xbc[..., D_SSD + SSD_GROUPS * SSD_STATE:].reshape(b, L, SSD_GROUPS, SSD_STATE).astype(F32)
    dt = jax.nn.softplus(dt_raw.astype(F32).reshape(b, L, 2, SSD_HEADS) + dt_bias.astype(F32))
    A = -jnp.exp(a_log.astype(F32))
    y_f = ssd_chunked(xs * dt[:, :, 0, :, None], dt[:, :, 0] * A[0], bm, cm)
    fl = lambda t: jnp.flip(t, axis=1)
    y_b = fl(ssd_chunked(fl(xs * dt[:, :, 1, :, None]), fl(dt[:, :, 1] * A[1]), fl(bm), fl(cm)))
    y = y_f + y_b + xs * d_skip.astype(F32)[:, None]
    y = y.reshape(b, L, D_SSD) * jax.nn.silu(z.astype(F32))
    yg = y.reshape(b, L, SSD_GROUPS, D_SSD // SSD_GROUPS)
    yg = yg * lax.rsqrt(jnp.mean(yg * yg, axis=-1, keepdims=True) + EPS)
    return (yg.reshape(b, L, D_SSD) * norm_w.astype(F32)).astype(z.dtype)


def shortconv_mixer(sc, conv_w, norm_w):
    bg, cg, hx = jnp.split(sc, 3, axis=-1)
    y = bg * dwconv(cg * hx, conv_w, SC_CONV // 2)
    return rmsnorm(y, norm_w)


def dilated_band_attention(q, k, v, dil, hw):
    b, S, h, dh = q.shape
    L = S // dil

    def to_sub(t):
        return jnp.swapaxes(t.reshape(b, L, dil, h, dh), 1, 2).reshape(b * dil, L, h, dh)

    def from_sub(t):
        t = t.reshape((b, dil) + t.shape[1:])
        return jnp.swapaxes(t, 1, 2).reshape((b, S) + t.shape[3:])

    qs, ks, vs = to_sub(q), to_sub(k), to_sub(v)
    nb = -(-L // ATT_QB)
    Lp = nb * ATT_QB
    kb_len = ATT_QB + 2 * hw
    qs = jnp.pad(qs, ((0, 0), (0, Lp - L), (0, 0), (0, 0))).reshape(b * dil, nb, ATT_QB, h, dh)
    pad_kv = ((0, 0), (hw, Lp - L + hw), (0, 0), (0, 0))
    kp, vp = jnp.pad(ks, pad_kv), jnp.pad(vs, pad_kv)
    kidx = jnp.arange(nb)[:, None] * ATT_QB + jnp.arange(kb_len)[None, :]
    kblk, vblk = kp[:, kidx], vp[:, kidx]
    s = jnp.einsum('bnqhd,bnkhd->bnhqk', qs, kblk).astype(F32) * (dh ** -0.5)
    key_pos = kidx - hw
    q_pos = jnp.arange(nb)[:, None] * ATT_QB + jnp.arange(ATT_QB)[None, :]
    rel = key_pos[:, None, :] - q_pos[:, :, None]
    valid = (jnp.abs(rel) <= hw) & (key_pos[:, None, :] >= 0) & (key_pos[:, None, :] < L)
    s = jnp.where(valid[None, :, None], s, NEG)
    lse = jax.nn.logsumexp(s, axis=-1)
    p = jnp.exp(s - lse[..., None])
    o = jnp.einsum('bnhqk,bnkhd->bnqhd', p, vblk.astype(F32))
    o = o.reshape(b * dil, Lp, h, dh)[:, :L]
    lse = jnp.swapaxes(lse, 2, 3).reshape(b * dil, Lp, h)[:, :L]
    return from_sub(o), from_sub(lse)


def attention_mixer(att, norm_w, cos, sin):
    b, L, _ = att.shape
    att = att.reshape(b, L, 3, N_DIL, ATT_SLOTS, HEAD_DIM)
    q = apply_rope(att[:, :, 0], cos, sin)
    k = apply_rope(att[:, :, 1], cos, sin)
    v = att[:, :, 2]
    outs, lses = [], []
    for g, (win, dil) in enumerate(DILATION_CFG):
        o, lse = dilated_band_attention(q[:, :, g], k[:, :, g], v[:, :, g], dil, win // (2 * dil))
        outs.append(o)
        lses.append(lse)
    w = jax.nn.softmax(jnp.stack(lses, axis=0), axis=0)
    o = jnp.einsum('gbls,gblsd->blsd', w, jnp.stack(outs, axis=0))
    return rmsnorm(o.reshape(b, L, D_ATT).astype(att.dtype), norm_w)


def peer_select(q, subkeys):
    T = q.shape[0]
    q = q.reshape(T, PEER_HEADS, 2, PEER_DQ // 2).astype(F32)
    s = jnp.einsum('thid,hind->thin', q, subkeys.astype(F32))
    sv, si = lax.top_k(s, PEER_TOPK)
    cand = (sv[:, :, 0, :, None] + sv[:, :, 1, None, :]).reshape(T, PEER_HEADS, PEER_TOPK * PEER_TOPK)
    cidx = (si[:, :, 0, :, None] * PEER_NKEYS + si[:, :, 1, None, :]).reshape(T, PEER_HEADS, PEER_TOPK * PEER_TOPK)
    top, pos = lax.top_k(cand, PEER_TOPK)
    eidx = jnp.take_along_axis(cidx, pos, axis=-1)
    gate = jax.nn.softmax(top, axis=-1)
    return eidx.reshape(T, PEER_HEADS * PEER_TOPK), gate.reshape(T, PEER_HEADS * PEER_TOPK)


def trunk(x, norm_mix, w_in, ssd_conv_w, ssd_conv_b, ssd_dt_bias, ssd_a_log, ssd_d, ssd_norm,
          sc_conv_w, sc_norm, att_norm, w_out, norm_ffn, peer_wq, peer_subkeys, peer_u, peer_v, norm_final):
    b, L, D = x.shape
    T = b * L
    cos, sin = rope_tables(L)
    x = x.reshape(T, D)
    o_xbc = D_SSD
    o_dt = o_xbc + XBC
    o_sc = o_dt + 2 * SSD_HEADS
    o_att = o_sc + 3 * D_SC
    for l in range(DEPTH):
        w = w_in[l]
        g = norm_mix[l]
        w_dt = jnp.pad(w[:, o_dt:o_sc], ((0, 0), (0, DT_PAD - 2 * SSD_HEADS)))
        z = norm_matmul(x, g, w[:, :o_xbc].astype(BF16))
        xbc = norm_matmul(x, g, w[:, o_xbc:o_dt].astype(BF16))
        dt_raw = norm_matmul(x, g, w_dt.astype(BF16))[:, :2 * SSD_HEADS]
        sc = norm_matmul(x, g, w[:, o_sc:o_att].astype(BF16), tn=768)
        att = norm_matmul(x, g, w[:, o_att:].astype(BF16), tn=768)
        r3 = lambda t: t.reshape(b, L, t.shape[-1])
        y = jnp.concatenate([
            ssd_mixer(r3(z), r3(xbc), r3(dt_raw), ssd_conv_w[l], ssd_conv_b[l], ssd_dt_bias[l],
                      ssd_a_log[l], ssd_d[l], ssd_norm[l]),
            shortconv_mixer(r3(sc), sc_conv_w[l], sc_norm[l]),
            attention_mixer(r3(att), att_norm[l], cos, sin)], axis=-1)
        x = matmul_residual(y.reshape(T, D_MIX), w_out[l].astype(BF16), x)
        q = norm_matmul(x, norm_ffn[l], peer_wq[l].astype(BF16))
        h = rmsnorm_rows(x, norm_ffn[l])
        eidx, gate = peer_select(q, peer_subkeys[l])
        x = peer_experts(h, x, eidx, gate, pack_table(peer_u[l]), pack_table(peer_v[l]),
                         tile=PEER_TILE, blk=PEER_BLK)
    return rmsnorm_rows(x, norm_final).reshape(b, L, D)


def kernel(x_prompt, x_sample, norm_mix, w_in, ssd_conv_w, ssd_conv_b, ssd_dt_bias, ssd_a_log, ssd_d, ssd_norm,
           sc_conv_w, sc_norm, att_norm, w_out, norm_ffn, peer_wq, peer_subkeys, peer_u, peer_v, norm_final):
    nb = x_prompt.shape[0]
    x = jnp.concatenate([x_prompt, x_sample], axis=0)
    y = trunk(x, norm_mix, w_in, ssd_conv_w, ssd_conv_b, ssd_dt_bias, ssd_a_log, ssd_d, ssd_norm,
              sc_conv_w, sc_norm, att_norm, w_out, norm_ffn, peer_wq, peer_subkeys, peer_u, peer_v, norm_final)
    return (y[:nb], y[nb:])
```

```python
import functools

import jax
import jax.numpy as jnp
import numpy as np
from jax import lax
from jax.experimental import pallas as pl
from jax.experimental.pallas import tpu as pltpu

D_MODEL = 2048
DEPTH = 2
D_MIX = 3 * D_MODEL // 2
HEAD_DIM = 64
D_SSD = D_MIX // 2
SSD_HEADS = D_SSD // HEAD_DIM
SSD_GROUPS = 4
SSD_STATE = 128
SSD_CONV = 5
SSD_CHUNK = 128
XBC = D_SSD + 2 * SSD_GROUPS * SSD_STATE
D_SC = D_MIX // 4
SC_CONV = 3
D_ATT = D_MIX // 4
ATT_SLOTS = D_ATT // HEAD_DIM
DILATION_CFG = ((128, 1), (512, 4), (2048, 16))
N_DIL = len(DILATION_CFG)
ATT_HEADS = N_DIL * ATT_SLOTS
ATT_QB = 64
ROPE_DIM = HEAD_DIM // 4
ROPE_THETA = 500000.0
PEER_HEADS = 8
PEER_NKEYS = 128
PEER_TOPK = 16
PEER_DQ = 256
EPS = 1e-6
NEG = -1e30
DT_PAD = 128

LANES = 128
SUBLANES = 8
D_ROWS = D_MODEL // LANES
P_ROWS = D_ROWS // 2
PEER_TILE = 4096
PEER_BLK = 128
PEER_CHUNK_SLOTS = 256
GROUP_UNROLL = 4
ROUTE_POS_SHIFT = 12
ROUTE_TOK_SHIFT = 20
ROUTE_TILE_SHIFT = 28

VMEM_LIMIT_BYTES = 56 * 1024 * 1024
PEER_VMEM_LIMIT = VMEM_LIMIT_BYTES

F32 = jnp.float32
BF16 = jnp.bfloat16


def _norm_matmul_kernel(x_ref, g_ref, w_ref, o_ref, h_scr):
    @pl.when(pl.program_id(1) == 0)
    def _():
        x = x_ref[...]
        ms = jnp.mean(x * x, axis=-1, keepdims=True)
        h_scr[...] = (x * lax.rsqrt(ms + EPS) * g_ref[...]).astype(BF16)

    o_ref[...] = jnp.dot(h_scr[...], w_ref[...], preferred_element_type=F32)


def norm_matmul(x, g, w, *, tm=512, tn=512):
    m, k = x.shape
    n = w.shape[1]
    tn = min(tn, n)
    assert m % tm == 0 and n % tn == 0
    return pl.pallas_call(
        _norm_matmul_kernel,
        grid=(m // tm, n // tn),
        in_specs=[
            pl.BlockSpec((tm, k), lambda i, j: (i, 0)),
            pl.BlockSpec((1, k), lambda i, j: (0, 0)),
            pl.BlockSpec((k, tn), lambda i, j: (0, j)),
        ],
        out_specs=pl.BlockSpec((tm, tn), lambda i, j: (i, j)),
        out_shape=jax.ShapeDtypeStruct((m, n), F32),
        scratch_shapes=[pltpu.VMEM((tm, k), BF16)],
        compiler_params=pltpu.CompilerParams(
            dimension_semantics=("parallel", "arbitrary"),
            vmem_limit_bytes=VMEM_LIMIT_BYTES),
        name="norm_matmul",
    )(x, g.reshape(1, k), w)


def _matmul_residual_kernel(y_ref, w_ref, r_ref, o_ref):
    o_ref[...] = r_ref[...] + jnp.dot(y_ref[...].astype(BF16), w_ref[...],
                                      preferred_element_type=F32)


def matmul_residual(y, w, res, *, tm=512, tn=512):
    m, k = y.shape
    n = w.shape[1]
    assert m % tm == 0 and n % tn == 0
    return pl.pallas_call(
        _matmul_residual_kernel,
        grid=(m // tm, n // tn),
        in_specs=[
            pl.BlockSpec((tm, k), lambda i, j: (i, 0)),
            pl.BlockSpec((k, tn), lambda i, j: (0, j)),
            pl.BlockSpec((tm, tn), lambda i, j: (i, j)),
        ],
        out_specs=pl.BlockSpec((tm, tn), lambda i, j: (i, j)),
        out_shape=jax.ShapeDtypeStruct((m, n), F32),
        compiler_params=pltpu.CompilerParams(
            dimension_semantics=("parallel", "arbitrary"),
            vmem_limit_bytes=VMEM_LIMIT_BYTES),
        name="matmul_residual",
    )(y, w, res)


def _rmsnorm_kernel(x_ref, g_ref, o_ref):
    x = x_ref[...]
    ms = jnp.mean(x * x, axis=-1, keepdims=True)
    o_ref[...] = x * lax.rsqrt(ms + EPS) * g_ref[...]


def rmsnorm_rows(x, g, *, tm=512):
    m, k = x.shape
    return pl.pallas_call(
        _rmsnorm_kernel,
        grid=(m // tm,),
        in_specs=[pl.BlockSpec((tm, k), lambda i: (i, 0)),
                  pl.BlockSpec((1, k), lambda i: (0, 0))],
        out_specs=pl.BlockSpec((tm, k), lambda i: (i, 0)),
        out_shape=jax.ShapeDtypeStruct((m, k), F32),
        compiler_params=pltpu.CompilerParams(dimension_semantics=("parallel",)),
        name="rmsnorm",
    )(x, g.reshape(1, k))


NCAND = PEER_TOPK * PEER_TOPK
ENC_SHIFT = PEER_NKEYS * PEER_NKEYS
CAND_AB = [(a, b) for a in range(PEER_TOPK) for b in range(PEER_TOPK) if (a + 1) * (b + 1) <= PEER_TOPK]
NCAND_KEPT = -(-len(CAND_AB) // SUBLANES) * SUBLANES
NSEL_EXT = 2 * PEER_TOPK + SUBLANES
NEG_INF = float("-inf")


def candidate_matrices():
    expand = np.zeros((NCAND_KEPT, NSEL_EXT), np.float32)
    scale = np.zeros((NCAND_KEPT, NSEL_EXT), np.float32)
    for j, (a, b) in enumerate(CAND_AB):
        expand[j, a] = expand[j, PEER_TOPK + b] = 1.0
        scale[j, a], scale[j, PEER_TOPK + b] = PEER_NKEYS, 1.0
        scale[j, 2 * PEER_TOPK] = (a * PEER_TOPK + b) * ENC_SHIFT
    expand[len(CAND_AB):, 2 * PEER_TOPK] = -1e30
    scale[len(CAND_AB):, 2 * PEER_TOPK] = (NCAND - 1) * ENC_SHIFT
    return jnp.asarray(expand, BF16), jnp.asarray(scale, BF16)


def _split3(x):
    p0 = x.astype(BF16)
    r = x - p0.astype(F32)
    p1 = r.astype(BF16)
    p2 = (r - p1.astype(F32)).astype(BF16)
    return p0, p1, p2


def _peer_topk_kernel(q_ref, k_ref, expand_ref, scale_ref, eidx_ref, gate_ref,
                      s_scr, sv_scr, si_scr, c_scr, enc_scr, top_scr, e_scr):
    tm = q_ref.shape[0]
    half = PEER_DQ // 2
    key_f = lax.broadcasted_iota(jnp.int32, (PEER_NKEYS, tm), 0).astype(F32)
    contract_last = (((1,), (1,)), ((), ()))

    for i in range(2):
        q = q_ref[:, i * half:(i + 1) * half].astype(BF16)
        s_scr[i] = lax.dot_general(k_ref[0, i], q, contract_last, preferred_element_type=F32)

    def round1(r, carry):
        for i in range(2):
            s = s_scr[i]
            m = jnp.max(s, axis=0, keepdims=True)
            pos = jnp.min(jnp.where(s == m, key_f, float(PEER_NKEYS)), axis=0, keepdims=True)
            s_scr[i] = jnp.where(key_f == pos, NEG_INF, s)
            sv_scr[pl.ds(i * PEER_TOPK + r, 1), :] = m
            si_scr[pl.ds(i * PEER_TOPK + r, 1), :] = pos
        return carry

    lax.fori_loop(0, PEER_TOPK, round1, 0)

    ones_row = (lax.broadcasted_iota(jnp.int32, (NSEL_EXT - 2 * PEER_TOPK, tm), 0) == 0).astype(F32)
    sv_scr[2 * PEER_TOPK:, :] = ones_row
    si_scr[2 * PEER_TOPK:, :] = ones_row
    c_scr[...] = sum(jnp.dot(expand_ref[...], p, preferred_element_type=F32) for p in _split3(sv_scr[...]))
    enc_scr[...] = jnp.dot(scale_ref[...], si_scr[...].astype(BF16), preferred_element_type=F32)
    big = float(NCAND * ENC_SHIFT)

    def round2(r, carry):
        c = c_scr[...]
        enc = enc_scr[...]
        m = jnp.max(c, axis=0, keepdims=True)
        e = jnp.min(jnp.where(c == m, enc, big), axis=0, keepdims=True)
        c_scr[...] = jnp.where(enc == e, NEG_INF, c)
        top_scr[pl.ds(r, 1), :] = m
        e_scr[pl.ds(r, 1), :] = e
        return carry

    lax.fori_loop(0, PEER_TOPK, round2, 0)
    top = top_scr[...]
    ex = jnp.exp(top - jnp.max(top, axis=0, keepdims=True))
    gate_ref[0] = ex / jnp.sum(ex, axis=0, keepdims=True)
    eidx_ref[0] = e_scr[...].astype(jnp.int32) & (ENC_SHIFT - 1)


def peer_topk(q, subkeys, *, tm=512):
    T = q.shape[0]
    assert T % tm == 0
    vm = pltpu.VMEM
    eidx, gate = pl.pallas_call(
        _peer_topk_kernel,
        grid=(T // tm, PEER_HEADS),
        in_specs=[pl.BlockSpec((tm, PEER_DQ), lambda i, h: (i, h)),
                  pl.BlockSpec((1, 2, PEER_NKEYS, PEER_DQ // 2), lambda i, h: (h, 0, 0, 0)),
                  pl.BlockSpec((NCAND_KEPT, NSEL_EXT), lambda i, h: (0, 0)),
                  pl.BlockSpec((NCAND_KEPT, NSEL_EXT), lambda i, h: (0, 0))],
        out_specs=[pl.BlockSpec((1, PEER_TOPK, tm), lambda i, h: (h, 0, i)),
                   pl.BlockSpec((1, PEER_TOPK, tm), lambda i, h: (h, 0, i))],
        out_shape=[jax.ShapeDtypeStruct((PEER_HEADS, PEER_TOPK, T), jnp.int32),
                   jax.ShapeDtypeStruct((PEER_HEADS, PEER_TOPK, T), F32)],
        scratch_shapes=[vm((2, PEER_NKEYS, tm), F32), vm((NSEL_EXT, tm), F32), vm((NSEL_EXT, tm), F32),
                        vm((NCAND_KEPT, tm), F32), vm((NCAND_KEPT, tm), F32), vm((PEER_TOPK, tm), F32),
                        vm((PEER_TOPK, tm), F32)],
        compiler_params=pltpu.CompilerParams(dimension_semantics=("parallel", "arbitrary")),
        name="peer_topk",
    )(q, subkeys.astype(BF16), *candidate_matrices())
    fix = lambda a: a.transpose(2, 0, 1).reshape(T, PEER_HEADS * PEER_TOPK)
    return fix(eidx), fix(gate)


def _fold(xa, xb, mask, shift):
    s = jnp.where(mask, xa, xb)
    t = jnp.where(mask, xb, xa)
    return s + pltpu.roll(t, shift, axis=0)


def _unpack(w):
    lo = lax.bitcast_convert_type(w << 16, F32)
    hi = lax.bitcast_convert_type(w & jnp.int32(-65536), F32)
    return lo, hi


def _peer_dot_kernel(meta_ref, *refs, chunk):
    row_refs = refs[:SUBLANES]
    gtok_ref, x_ref, gate_ref, u_ref, a_ref, c_scr = refs[SUBLANES:]
    k = pl.program_id(1)
    gpc = chunk // SUBLANES

    @pl.when(k == 0)
    def _():
        a_ref[...] = jnp.zeros_like(a_ref)

    row_i = lax.broadcasted_iota(jnp.int32, (SUBLANES, LANES), 0)
    m1 = (row_i & 1) == 0
    m2 = (row_i & 2) == 0
    m3 = (row_i & 4) == 0
    ones = jnp.ones((SUBLANES, LANES), BF16)

    def chunk_body(c, carry):
        def groups(gi, carry):
            for sub in range(GROUP_UNROLL):
                gl = gi * GROUP_UNROLL + sub
                g = c * gpc + gl
                tok = gtok_ref[0, 0, g]
                xlo = x_ref[tok, 0:P_ROWS, :]
                xhi = x_ref[tok, P_ROWS:D_ROWS, :]
                ps = []
                for i in range(SUBLANES):
                    r8 = pl.multiple_of(row_refs[i][0, 0, g], SUBLANES)
                    lo, hi = _unpack(u_ref[pl.ds(r8, SUBLANES), :])
                    ps.append(lo * xlo + hi * xhi)
                q = [_fold(ps[2 * i], ps[2 * i + 1], m1, 1) for i in range(4)]
                r = [_fold(q[0], q[1], m2, 2), _fold(q[2], q[3], m2, 2)]
                c_scr[pl.ds(pl.multiple_of(gl * SUBLANES, SUBLANES), SUBLANES), :] = _fold(r[0], r[1], m3, 4)
            return carry

        lax.fori_loop(0, gpc // GROUP_UNROLL, groups, 0)
        dims = (((1,), (1,)), ((), ()))
        h8 = sum(lax.dot_general(ones, p, dims, preferred_element_type=F32) for p in _split3(c_scr[...]))
        h = h8[0:1, :]
        off = pl.multiple_of(c * chunk, chunk)
        act = 0.5 * h * (1.0 + lax.erf(h * 0.7071067811865476))
        a_ref[0, :, pl.ds(off, chunk)] = act * gate_ref[0, :, pl.ds(off, chunk)]
        return carry

    lax.fori_loop(meta_ref[0, 0, k], meta_ref[0, 0, k + 1], chunk_body, 0)


def _peer_acc_kernel(meta_ref, *refs, chunk):
    row_refs = refs[:SUBLANES]
    a_refs = refs[SUBLANES:2 * SUBLANES]
    gtok_ref, res_ref, v_ref, o_ref = refs[2 * SUBLANES:]
    k = pl.program_id(1)
    gpc = chunk // SUBLANES

    @pl.when(k == 0)
    def _():
        o_ref[...] = res_ref[...]

    def groups(gp, carry):
        for sub in range(GROUP_UNROLL):
            g = gp * GROUP_UNROLL + sub
            tok = gtok_ref[0, 0, g]
            alo = ahi = None
            for i in range(SUBLANES):
                r8 = pl.multiple_of(row_refs[i][0, 0, g], SUBLANES)
                lo, hi = _unpack(v_ref[pl.ds(r8, SUBLANES), :])
                a = a_refs[i][0, 0, g]
                alo = a * lo if alo is None else alo + a * lo
                ahi = a * hi if ahi is None else ahi + a * hi
            o_ref[tok, 0:P_ROWS, :] += alo
            o_ref[tok, P_ROWS:D_ROWS, :] += ahi
        return carry

    per_chunk = gpc // GROUP_UNROLL
    lax.fori_loop(meta_ref[0, 0, k] * per_chunk, meta_ref[0, 0, k + 1] * per_chunk, groups, 0)


def pack_table(w):
    e = w.shape[0]
    bits = lax.bitcast_convert_type(w.astype(BF16), jnp.uint16).astype(jnp.uint32)
    half = D_MODEL // 2
    packed = (bits[:, half:] << 16) | bits[:, :half]
    return lax.bitcast_convert_type(packed, jnp.int32).reshape(e * P_ROWS, LANES)


def by_lane(a):
    nb, cap = a.shape[0], a.shape[-1]
    a = a.reshape(nb, cap // SUBLANES, SUBLANES)
    return [a[:, :, i].reshape(nb, 1, cap // SUBLANES) for i in range(SUBLANES)]


def peer_route(eidx, gate, *, n_exp, tile, blk, chunk):
    T, S = eidx.shape
    nb = T // blk
    nt = n_exp // tile
    shift = tile.bit_length() - 1
    assert 1 << shift == tile and T % blk == 0 and S + SUBLANES <= 256 and chunk <= 256
    assert blk & (blk - 1) == 0 and blk < 256 and shift <= ROUTE_POS_SHIFT and nt <= 4
    i32 = jnp.int32
    tiles = jnp.arange(nt, dtype=i32)
    tile_id = eidx >> shift
    cnt = jnp.sum(tile_id[:, :, None] == tiles[None, None, :], axis=1, dtype=i32)
    tpad = (-cnt) % SUBLANES
    t_in = (jnp.arange(T, dtype=i32) % blk)[:, None]
    dead = i32(nt << ROUTE_TILE_SHIFT)
    tile_f, tok_f, pos_f = ROUTE_TILE_SHIFT, ROUTE_TOK_SHIFT, ROUTE_POS_SHIFT
    key_real = ((tile_id << tile_f) | (t_in << tok_f) | (jnp.arange(S, dtype=i32)[None, :] << pos_f)
                | (eidx & (tile - 1)))
    i8 = jnp.arange(SUBLANES, dtype=i32)
    key_tpad = jnp.where(i8[None, None, :] < tpad[:, :, None],
                         (tiles[None, :, None] << tile_f) | (t_in[:, :, None] << tok_f)
                         | ((S + i8)[None, None, :] << pos_f), dead)
    n_run = jnp.sum((cnt + tpad).reshape(nb, blk, nt), axis=1)
    bpad = (-n_run) % chunk
    ib = jnp.arange(chunk - SUBLANES, dtype=i32)
    key_bpad = jnp.where(ib[None, None, :] < bpad[:, :, None],
                         (tiles[None, :, None] << tile_f) | i32(blk << tok_f) | (ib[None, None, :] << pos_f), dead)
    n_real, n_tp, n_bp = blk * S, blk * nt * SUBLANES, nt * (chunk - SUBLANES)
    cap = -(-(n_real + n_tp + nt * chunk) // 1024) * 1024
    fill = cap - (n_real + n_tp + n_bp)
    keys = jnp.concatenate([key_real.reshape(nb, n_real), key_tpad.reshape(nb, n_tp), key_bpad.reshape(nb, n_bp),
                            jnp.full((nb, fill), dead, i32)], axis=1)
    gates = jnp.concatenate([gate.reshape(nb, n_real), jnp.zeros((nb, cap - n_real), gate.dtype)], axis=1)
    keys, gates = lax.sort((keys, gates), dimension=1, num_keys=1)
    rows = (keys & ((1 << pos_f) - 1)) * SUBLANES
    gtok = (keys[:, ::SUBLANES] >> tok_f) & (blk - 1)
    n_chunks = (n_run + bpad) // chunk
    cstart = jnp.concatenate([jnp.zeros((nb, 1), i32), jnp.cumsum(n_chunks, axis=1)], axis=1)
    meta = jnp.pad(cstart, ((0, 0), (0, LANES - nt - 1)))
    r3 = lambda a: a.reshape(nb, 1, a.shape[1])
    return r3(meta), by_lane(rows), r3(gtok), r3(gates), cap


def peer_experts(h, res, eidx, gate, u_packed, v_packed, *, tile, blk):
    T = h.shape[0]
    chunk = PEER_CHUNK_SLOTS
    n_exp = u_packed.shape[0] // P_ROWS
    nt = n_exp // tile
    nb = T // blk
    meta, rows, gtok, gates, cap = peer_route(eidx, gate, n_exp=n_exp, tile=tile, blk=blk, chunk=chunk)
    tok_spec = pl.BlockSpec((blk, D_ROWS, LANES), lambda j, k: (j, 0, 0))
    tab_spec = pl.BlockSpec((tile * P_ROWS, LANES), lambda j, k: (k, 0))
    smem = lambda n: pl.BlockSpec((1, 1, n), lambda j, k: (j, 0, 0), memory_space=pltpu.SMEM,
                                  pipeline_mode=pl.Buffered(1))
    slot_spec = pl.BlockSpec((1, 1, cap), lambda j, k: (j, 0, 0))
    params = pltpu.CompilerParams(dimension_semantics=("parallel", "arbitrary"), vmem_limit_bytes=PEER_VMEM_LIMIT)
    a = pl.pallas_call(
        functools.partial(_peer_dot_kernel, chunk=chunk),
        grid=(nb, nt),
        in_specs=[smem(LANES)] + [smem(cap // SUBLANES)] * (SUBLANES + 1) + [tok_spec, slot_spec, tab_spec],
        out_specs=slot_spec,
        out_shape=jax.ShapeDtypeStruct((nb, 1, cap), F32),
        scratch_shapes=[pltpu.VMEM((chunk, LANES), F32)],
        compiler_params=params, name="peer_dot",
    )(meta, *rows, gtok, h.reshape(T, D_ROWS, LANES), gates, u_packed)
    out = pl.pallas_call(
        functools.partial(_peer_acc_kernel, chunk=chunk),
        grid=(nb, nt),
        in_specs=[smem(LANES)] + [smem(cap // SUBLANES)] * (2 * SUBLANES + 1) + [tok_spec, tab_spec],
        out_specs=tok_spec,
        out_shape=jax.ShapeDtypeStruct((T, D_ROWS, LANES), F32),
        compiler_params=params, name="peer_acc",
    )(meta, *rows, *by_lane(a.reshape(nb, cap)), gtok, res.reshape(T, D_ROWS, LANES), v_packed)
    return out.reshape(T, D_MODEL)


def rmsnorm(x, g):
    xf = x.astype(F32)
    y = xf * lax.rsqrt(jnp.mean(xf * xf, axis=-1, keepdims=True) + EPS) * g.astype(F32)
    return y.astype(x.dtype)


def dwconv(x, w, pad):
    c = x.shape[-1]
    return lax.conv_general_dilated(x, w[:, None, :].astype(x.dtype), window_strides=(1,),
                                    padding=[(pad, pad)], dimension_numbers=('NWC', 'WIO', 'NWC'),
                                    feature_group_count=c)


def rope_tables(length):
    inv = ROPE_THETA ** (-jnp.arange(0, ROPE_DIM, 2, dtype=F32) / ROPE_DIM)
    ang = jnp.arange(length, dtype=F32)[:, None] * inv[None, :]
    return jnp.cos(ang), jnp.sin(ang)


def apply_rope(t, cos, sin):
    tf = t.astype(F32)
    half = ROPE_DIM // 2
    c = cos[None, :, None, None, :]
    s = sin[None, :, None, None, :]
    x1 = tf[..., :half]
    x2 = tf[..., half:ROPE_DIM]
    out = jnp.concatenate([x1 * c - x2 * s, x2 * c + x1 * s, tf[..., ROPE_DIM:]], axis=-1)
    return out.astype(t.dtype)


def ssd_chunked(x, a, bm, cm):
    b, L, H, P = x.shape
    G, N = bm.shape[2], bm.shape[3]
    R = H // G
    Q = SSD_CHUNK
    c = L // Q
    x = x.reshape(b, c, Q, G, R, P)
    a = a.reshape(b, c, Q, G, R)
    bm = bm.reshape(b, c, Q, G, N)
    cm = cm.reshape(b, c, Q, G, N)
    acum = jnp.cumsum(a, axis=2)
    lower = jnp.tril(jnp.ones((Q, Q), dtype=bool))
    seg = acum[:, :, :, None] - acum[:, :, None, :]
    decay = jnp.exp(jnp.where(lower[None, None, :, :, None, None], seg, -jnp.inf))
    cb = jnp.einsum('bclgn,bcsgn->bclsg', cm, bm)
    y_diag = jnp.einsum('bclsgr,bcsgrp->bclgrp', cb[..., None] * decay, x)
    decay_end = jnp.exp(acum[:, :, -1:] - acum)
    states = jnp.einsum('bclgn,bclgrp->bcgrpn', bm, x * decay_end[..., None])
    chunk_decay = jnp.exp(acum[:, :, -1])

    def step(h, inp):
        dec, st = inp
        return dec[..., None, None] * h + st, h

    h0 = jnp.zeros((b, G, R, P, N), x.dtype)
    _, h_in = lax.scan(step, h0, (jnp.moveaxis(chunk_decay, 1, 0), jnp.moveaxis(states, 1, 0)))
    h_in = jnp.moveaxis(h_in, 0, 1)
    y_off = jnp.einsum('bclgn,bcgrpn->bclgrp', cm, h_in) * jnp.exp(acum)[..., None]
    return (y_diag + y_off).reshape(b, L, H, P)


def ssd_mixer(z, xbc, dt_raw, conv_w, conv_b, dt_bias, a_log, d_skip, norm_w):
    b, L, _ = z.shape
    xbc = jax.nn.silu(dwconv(xbc, conv_w, SSD_CONV // 2) + conv_b.astype(xbc.dtype))
    xs = xbc[..., :D_SSD].reshape(b, L, SSD_HEADS, HEAD_DIM).astype(F32)
    bm = xbc[..., D_SSD:D_SSD + SSD_GROUPS * SSD_STATE].reshape(b, L, SSD_GROUPS, SSD_STATE).astype(F32)
    cm = xbc[..., D_SSD + SSD_GROUPS * SSD_STATE:].reshape(b, L, SSD_GROUPS, SSD_STATE).astype(F32)
    dt = jax.nn.softplus(dt_raw.astype(F32).reshape(b, L, 2, SSD_HEADS) + dt_bias.astype(F32))
    A = -jnp.exp(a_log.astype(F32))
    y_f = ssd_chunked(xs * dt[:, :, 0, :, None], dt[:, :, 0] * A[0], bm, cm)
    fl = lambda t: jnp.flip(t, axis=1)
    y_b = fl(ssd_chunked(fl(xs * dt[:, :, 1, :, None]), fl(dt[:, :, 1] * A[1]), fl(bm), fl(cm)))
    y = y_f + y_b + xs * d_skip.astype(F32)[:, None]
    y = y.reshape(b, L, D_SSD) * jax.nn.silu(z.astype(F32))
    yg = y.reshape(b, L, SSD_GROUPS, D_SSD // SSD_GROUPS)
    yg = yg * lax.rsqrt(jnp.mean(yg * yg, axis=-1, keepdims=True) + EPS)
    return (yg.reshape(b, L, D_SSD) * norm_w.astype(F32)).astype(z.dtype)


def shortconv_mixer(sc, conv_w, norm_w):
    bg, cg, hx = jnp.split(sc, 3, axis=-1)
    y = bg * dwconv(cg * hx, conv_w, SC_CONV // 2)
    return rmsnorm(y, norm_w)


def dilated_band_attention(q, k, v, dil, hw):
    b, S, h, dh = q.shape
    L = S // dil

    def to_sub(t):
        return jnp.swapaxes(t.reshape(b, L, dil, h, dh), 1, 2).reshape(b * dil, L, h, dh)

    def from_sub(t):
        t = t.reshape((b, dil) + t.shape[1:])
        return jnp.swapaxes(t, 1, 2).reshape((b, S) + t.shape[3:])

    qs, ks, vs = to_sub(q), to_sub(k), to_sub(v)
    nb = -(-L // ATT_QB)
    Lp = nb * ATT_QB
    kb_len = ATT_QB + 2 * hw
    qs = jnp.pad(qs, ((0, 0), (0, Lp - L), (0, 0), (0, 0))).reshape(b * dil, nb, ATT_QB, h, dh)
    pad_kv = ((0, 0), (hw, Lp - L + hw), (0, 0), (0, 0))
    kp, vp = jnp.pad(ks, pad_kv), jnp.pad(vs, pad_kv)
    kidx = jnp.arange(nb)[:, None] * ATT_QB + jnp.arange(kb_len)[None, :]
    kblk, vblk = kp[:, kidx], vp[:, kidx]
    s = jnp.einsum('bnqhd,bnkhd->bnhqk', qs, kblk).astype(F32) * (dh ** -0.5)
    key_pos = kidx - hw
    q_pos = jnp.arange(nb)[:, None] * ATT_QB + jnp.arange(ATT_QB)[None, :]
    rel = key_pos[:, None, :] - q_pos[:, :, None]
    valid = (jnp.abs(rel) <= hw) & (key_pos[:, None, :] >= 0) & (key_pos[:, None, :] < L)
    s = jnp.where(valid[None, :, None], s, NEG)
    lse = jax.nn.logsumexp(s, axis=-1)
    p = jnp.exp(s - lse[..., None])
    o = jnp.einsum('bnhqk,bnkhd->bnqhd', p, vblk.astype(F32))
    o = o.reshape(b * dil, Lp, h, dh)[:, :L]
    lse = jnp.swapaxes(lse, 2, 3).reshape(b * dil, Lp, h)[:, :L]
    return from_sub(o), from_sub(lse)


def attention_mixer(att, norm_w, cos, sin):
    b, L, _ = att.shape
    att = att.reshape(b, L, 3, N_DIL, ATT_SLOTS, HEAD_DIM)
    q = apply_rope(att[:, :, 0], cos, sin)
    k = apply_rope(att[:, :, 1], cos, sin)
    v = att[:, :, 2]
    outs, lses = [], []
    for g, (win, dil) in enumerate(DILATION_CFG):
        o, lse = dilated_band_attention(q[:, :, g], k[:, :, g], v[:, :, g], dil, win // (2 * dil))
        outs.append(o)
        lses.append(lse)
    w = jax.nn.softmax(jnp.stack(lses, axis=0), axis=0)
    o = jnp.einsum('gbls,gblsd->blsd', w, jnp.stack(outs, axis=0))
    return rmsnorm(o.reshape(b, L, D_ATT).astype(att.dtype), norm_w)


def trunk(x, norm_mix, w_in, ssd_conv_w, ssd_conv_b, ssd_dt_bias, ssd_a_log, ssd_d, ssd_norm,
          sc_conv_w, sc_norm, att_norm, w_out, norm_ffn, peer_wq, peer_subkeys, peer_u, peer_v, norm_final):
    b, L, D = x.shape
    T = b * L
    cos, sin = rope_tables(L)
    x = x.reshape(T, D)
    o_xbc = D_SSD
    o_dt = o_xbc + XBC
    o_sc = o_dt + 2 * SSD_HEADS
    o_att = o_sc + 3 * D_SC
    for l in range(DEPTH):
        w = w_in[l]
        g = norm_mix[l]
        w_dt = jnp.pad(w[:, o_dt:o_sc], ((0, 0), (0, DT_PAD - 2 * SSD_HEADS)))
        z = norm_matmul(x, g, w[:, :o_xbc].astype(BF16))
        xbc = norm_matmul(x, g, w[:, o_xbc:o_dt].astype(BF16))
        dt_raw = norm_matmul(x, g, w_dt.astype(BF16))[:, :2 * SSD_HEADS]
        sc = norm_matmul(x, g, w[:, o_sc:o_att].astype(BF16), tn=768)
        att = norm_matmul(x, g, w[:, o_att:].astype(BF16), tn=768)
        r3 = lambda t: t.reshape(b, L, t.shape[-1])
        y = jnp.concatenate([
            ssd_mixer(r3(z), r3(xbc), r3(dt_raw), ssd_conv_w[l], ssd_conv_b[l], ssd_dt_bias[l],
                      ssd_a_log[l], ssd_d[l], ssd_norm[l]),
            shortconv_mixer(r3(sc), sc_conv_w[l], sc_norm[l]),
            attention_mixer(r3(att), att_norm[l], cos, sin)], axis=-1)
        x = matmul_residual(y.reshape(T, D_MIX), w_out[l].astype(BF16), x)
        q = norm_matmul(x, norm_ffn[l], peer_wq[l].astype(BF16))
        h = rmsnorm_rows(x, norm_ffn[l])
        eidx, gate = peer_topk(q, peer_subkeys[l])
        x = peer_experts(h, x, eidx, gate, pack_table(peer_u[l]), pack_table(peer_v[l]),
                         tile=PEER_TILE, blk=PEER_BLK)
    return rmsnorm_rows(x, norm_final).reshape(b, L, D)


def kernel(x_prompt, x_sample, norm_mix, w_in, ssd_conv_w, ssd_conv_b, ssd_dt_bias, ssd_a_log, ssd_d, ssd_norm,
           sc_conv_w, sc_norm, att_norm, w_out, norm_ffn, peer_wq, peer_subkeys, peer_u, peer_v, norm_final):
    nb = x_prompt.shape[0]
    x = jnp.concatenate([x_prompt, x_sample], axis=0)
    y = trunk(x, norm_mix, w_in, ssd_conv_w, ssd_conv_b, ssd_dt_bias, ssd_a_log, ssd_d, ssd_norm,
              sc_conv_w, sc_norm, att_norm, w_out, norm_ffn, peer_wq, peer_subkeys, peer_u, peer_v, norm_final)
    return (y[:nb], y[nb:])
```

```python
import functools

import jax
import jax.numpy as jnp
import numpy as np
from jax import lax
from jax.experimental import pallas as pl
from jax.experimental.pallas import tpu as pltpu

D_MODEL = 2048
DEPTH = 2
D_MIX = 3 * D_MODEL // 2
HEAD_DIM = 64
D_SSD = D_MIX // 2
SSD_HEADS = D_SSD // HEAD_DIM
SSD_GROUPS = 4
SSD_STATE = 128
SSD_CONV = 5
SSD_CHUNK = 128
XBC = D_SSD + 2 * SSD_GROUPS * SSD_STATE
D_SC = D_MIX // 4
SC_CONV = 3
D_ATT = D_MIX // 4
DILATION_CFG = ((128, 1), (512, 4), (2048, 16))
N_DIL = len(DILATION_CFG)
ROPE_DIM = HEAD_DIM // 4
ROPE_THETA = 500000.0
PEER_HEADS = 8
PEER_NKEYS = 128
PEER_TOPK = 16
PEER_DQ = 256
EPS = 1e-6
NEG = -1e30
DT_PAD = 128

LANES = 128
SUBLANES = 8
D_ROWS = D_MODEL // LANES
P_ROWS = D_ROWS // 2
PEER_TILE = 4096
PEER_BLK = 128
PEER_CHUNK_SLOTS = 256
GROUP_UNROLL = 4
ROUTE_POS_SHIFT = 12
ROUTE_TOK_SHIFT = 20
ROUTE_TILE_SHIFT = 28

VMEM_LIMIT_BYTES = 56 * 1024 * 1024
PEER_VMEM_LIMIT = VMEM_LIMIT_BYTES

F32 = jnp.float32
BF16 = jnp.bfloat16


def _norm_matmul_kernel(x_ref, g_ref, w_ref, o_ref, h_scr):
    @pl.when(pl.program_id(1) == 0)
    def _():
        x = x_ref[...]
        ms = jnp.mean(x * x, axis=-1, keepdims=True)
        h_scr[...] = (x * lax.rsqrt(ms + EPS) * g_ref[...]).astype(BF16)

    o_ref[...] = jnp.dot(h_scr[...], w_ref[...], preferred_element_type=F32)


def norm_matmul(x, g, w, *, tm=512, tn=512):
    m, k = x.shape
    n = w.shape[1]
    tn = min(tn, n)
    assert m % tm == 0 and n % tn == 0
    return pl.pallas_call(
        _norm_matmul_kernel,
        grid=(m // tm, n // tn),
        in_specs=[
            pl.BlockSpec((tm, k), lambda i, j: (i, 0)),
            pl.BlockSpec((1, k), lambda i, j: (0, 0)),
            pl.BlockSpec((k, tn), lambda i, j: (0, j)),
        ],
        out_specs=pl.BlockSpec((tm, tn), lambda i, j: (i, j)),
        out_shape=jax.ShapeDtypeStruct((m, n), F32),
        scratch_shapes=[pltpu.VMEM((tm, k), BF16)],
        compiler_params=pltpu.CompilerParams(
            dimension_semantics=("parallel", "arbitrary"),
            vmem_limit_bytes=VMEM_LIMIT_BYTES),
        name="norm_matmul",
    )(x, g.reshape(1, k), w)


def _matmul_residual_kernel(y_ref, w_ref, r_ref, o_ref):
    o_ref[...] = r_ref[...] + jnp.dot(y_ref[...].astype(BF16), w_ref[...],
                                      preferred_element_type=F32)


def matmul_residual(y, w, res, *, tm=512, tn=512):
    m, k = y.shape
    n = w.shape[1]
    assert m % tm == 0 and n % tn == 0
    return pl.pallas_call(
        _matmul_residual_kernel,
        grid=(m // tm, n // tn),
        in_specs=[
            pl.BlockSpec((tm, k), lambda i, j: (i, 0)),
            pl.BlockSpec((k, tn), lambda i, j: (0, j)),
            pl.BlockSpec((tm, tn), lambda i, j: (i, j)),
        ],
        out_specs=pl.BlockSpec((tm, tn), lambda i, j: (i, j)),
        out_shape=jax.ShapeDtypeStruct((m, n), F32),
        compiler_params=pltpu.CompilerParams(
            dimension_semantics=("parallel", "arbitrary"),
            vmem_limit_bytes=VMEM_LIMIT_BYTES),
        name="matmul_residual",
    )(y, w, res)


def _rmsnorm_kernel(x_ref, g_ref, o_ref):
    x = x_ref[...]
    ms = jnp.mean(x * x, axis=-1, keepdims=True)
    o_ref[...] = x * lax.rsqrt(ms + EPS) * g_ref[...]


def rmsnorm_rows(x, g, *, tm=512):
    m, k = x.shape
    return pl.pallas_call(
        _rmsnorm_kernel,
        grid=(m // tm,),
        in_specs=[pl.BlockSpec((tm, k), lambda i: (i, 0)),
                  pl.BlockSpec((1, k), lambda i: (0, 0))],
        out_specs=pl.BlockSpec((tm, k), lambda i: (i, 0)),
        out_shape=jax.ShapeDtypeStruct((m, k), F32),
        compiler_params=pltpu.CompilerParams(dimension_semantics=("parallel",)),
        name="rmsnorm",
    )(x, g.reshape(1, k))


NCAND = PEER_TOPK * PEER_TOPK
ENC_SHIFT = PEER_NKEYS * PEER_NKEYS
CAND_AB = [(a, b) for a in range(PEER_TOPK) for b in range(PEER_TOPK) if (a + 1) * (b + 1) <= PEER_TOPK]
NCAND_KEPT = -(-len(CAND_AB) // SUBLANES) * SUBLANES
NSEL_EXT = 2 * PEER_TOPK + SUBLANES
NEG_INF = float("-inf")


def candidate_matrices():
    expand = np.zeros((NCAND_KEPT, NSEL_EXT), np.float32)
    scale = np.zeros((NCAND_KEPT, NSEL_EXT), np.float32)
    for j, (a, b) in enumerate(CAND_AB):
        expand[j, a] = expand[j, PEER_TOPK + b] = 1.0
        scale[j, a], scale[j, PEER_TOPK + b] = PEER_NKEYS, 1.0
        scale[j, 2 * PEER_TOPK] = (a * PEER_TOPK + b) * ENC_SHIFT
    expand[len(CAND_AB):, 2 * PEER_TOPK] = -1e30
    scale[len(CAND_AB):, 2 * PEER_TOPK] = (NCAND - 1) * ENC_SHIFT
    return jnp.asarray(expand, BF16), jnp.asarray(scale, BF16)


def _split3(x):
    p0 = x.astype(BF16)
    r = x - p0.astype(F32)
    p1 = r.astype(BF16)
    p2 = (r - p1.astype(F32)).astype(BF16)
    return p0, p1, p2


def _peer_topk_kernel(q_ref, k_ref, expand_ref, scale_ref, eidx_ref, gate_ref,
                      s_scr, sv_scr, si_scr, c_scr, enc_scr, top_scr, e_scr):
    tm = q_ref.shape[0]
    half = PEER_DQ // 2
    key_f = lax.broadcasted_iota(jnp.int32, (PEER_NKEYS, tm), 0).astype(F32)
    contract_last = (((1,), (1,)), ((), ()))

    for i in range(2):
        q = q_ref[:, i * half:(i + 1) * half].astype(BF16)
        s_scr[i] = lax.dot_general(k_ref[0, i], q, contract_last, preferred_element_type=F32)

    def round1(r, carry):
        for i in range(2):
            s = s_scr[i]
            m = jnp.max(s, axis=0, keepdims=True)
            pos = jnp.min(jnp.where(s == m, key_f, float(PEER_NKEYS)), axis=0, keepdims=True)
            s_scr[i] = jnp.where(key_f == pos, NEG_INF, s)
            sv_scr[pl.ds(i * PEER_TOPK + r, 1), :] = m
            si_scr[pl.ds(i * PEER_TOPK + r, 1), :] = pos
        return carry

    lax.fori_loop(0, PEER_TOPK, round1, 0)

    ones_row = (lax.broadcasted_iota(jnp.int32, (NSEL_EXT - 2 * PEER_TOPK, tm), 0) == 0).astype(F32)
    sv_scr[2 * PEER_TOPK:, :] = ones_row
    si_scr[2 * PEER_TOPK:, :] = ones_row
    c_scr[...] = sum(jnp.dot(expand_ref[...], p, preferred_element_type=F32) for p in _split3(sv_scr[...]))
    enc_scr[...] = jnp.dot(scale_ref[...], si_scr[...].astype(BF16), preferred_element_type=F32)
    big = float(NCAND * ENC_SHIFT)

    def round2(r, carry):
        c = c_scr[...]
        enc = enc_scr[...]
        m = jnp.max(c, axis=0, keepdims=True)
        e = jnp.min(jnp.where(c == m, enc, big), axis=0, keepdims=True)
        c_scr[...] = jnp.where(enc == e, NEG_INF, c)
        top_scr[pl.ds(r, 1), :] = m
        e_scr[pl.ds(r, 1), :] = e
        return carry

    lax.fori_loop(0, PEER_TOPK, round2, 0)
    top = top_scr[...]
    ex = jnp.exp(top - jnp.max(top, axis=0, keepdims=True))
    gate_ref[0] = ex / jnp.sum(ex, axis=0, keepdims=True)
    eidx_ref[0] = e_scr[...].astype(jnp.int32) & (ENC_SHIFT - 1)


def peer_topk(q, subkeys, *, tm=512):
    T = q.shape[0]
    assert T % tm == 0
    vm = pltpu.VMEM
    eidx, gate = pl.pallas_call(
        _peer_topk_kernel,
        grid=(T // tm, PEER_HEADS),
        in_specs=[pl.BlockSpec((tm, PEER_DQ), lambda i, h: (i, h)),
                  pl.BlockSpec((1, 2, PEER_NKEYS, PEER_DQ // 2), lambda i, h: (h, 0, 0, 0)),
                  pl.BlockSpec((NCAND_KEPT, NSEL_EXT), lambda i, h: (0, 0)),
                  pl.BlockSpec((NCAND_KEPT, NSEL_EXT), lambda i, h: (0, 0))],
        out_specs=[pl.BlockSpec((1, PEER_TOPK, tm), lambda i, h: (h, 0, i)),
                   pl.BlockSpec((1, PEER_TOPK, tm), lambda i, h: (h, 0, i))],
        out_shape=[jax.ShapeDtypeStruct((PEER_HEADS, PEER_TOPK, T), jnp.int32),
                   jax.ShapeDtypeStruct((PEER_HEADS, PEER_TOPK, T), F32)],
        scratch_shapes=[vm((2, PEER_NKEYS, tm), F32), vm((NSEL_EXT, tm), F32), vm((NSEL_EXT, tm), F32),
                        vm((NCAND_KEPT, tm), F32), vm((NCAND_KEPT, tm), F32), vm((PEER_TOPK, tm), F32),
                        vm((PEER_TOPK, tm), F32)],
        compiler_params=pltpu.CompilerParams(dimension_semantics=("parallel", "arbitrary")),
        name="peer_topk",
    )(q, subkeys.astype(BF16), *candidate_matrices())
    fix = lambda a: a.transpose(2, 0, 1).reshape(T, PEER_HEADS * PEER_TOPK)
    return fix(eidx), fix(gate)


def _fold(xa, xb, mask, shift):
    s = jnp.where(mask, xa, xb)
    t = jnp.where(mask, xb, xa)
    return s + pltpu.roll(t, shift, axis=0)


def _unpack(w):
    lo = lax.bitcast_convert_type(w << 16, F32)
    hi = lax.bitcast_convert_type(w & jnp.int32(-65536), F32)
    return lo, hi


def _peer_dot_kernel(meta_ref, *refs, chunk):
    row_refs = refs[:SUBLANES]
    gtok_ref, x_ref, gate_ref, u_ref, a_ref, c_scr = refs[SUBLANES:]
    k = pl.program_id(1)
    gpc = chunk // SUBLANES

    @pl.when(k == 0)
    def _():
        a_ref[...] = jnp.zeros_like(a_ref)

    row_i = lax.broadcasted_iota(jnp.int32, (SUBLANES, LANES), 0)
    m1 = (row_i & 1) == 0
    m2 = (row_i & 2) == 0
    m3 = (row_i & 4) == 0
    ones = jnp.ones((SUBLANES, LANES), BF16)

    def chunk_body(c, carry):
        def groups(gi, carry):
            for sub in range(GROUP_UNROLL):
                gl = gi * GROUP_UNROLL + sub
                g = c * gpc + gl
                tok = gtok_ref[0, 0, g]
                xlo = x_ref[tok, 0:P_ROWS, :]
                xhi = x_ref[tok, P_ROWS:D_ROWS, :]
                ps = []
                for i in range(SUBLANES):
                    r8 = pl.multiple_of(row_refs[i][0, 0, g], SUBLANES)
                    lo, hi = _unpack(u_ref[pl.ds(r8, SUBLANES), :])
                    ps.append(lo * xlo + hi * xhi)
                q = [_fold(ps[2 * i], ps[2 * i + 1], m1, 1) for i in range(4)]
                r = [_fold(q[0], q[1], m2, 2), _fold(q[2], q[3], m2, 2)]
                c_scr[pl.ds(pl.multiple_of(gl * SUBLANES, SUBLANES), SUBLANES), :] = _fold(r[0], r[1], m3, 4)
            return carry

        lax.fori_loop(0, gpc // GROUP_UNROLL, groups, 0)
        dims = (((1,), (1,)), ((), ()))
        h8 = sum(lax.dot_general(ones, p, dims, preferred_element_type=F32) for p in _split3(c_scr[...]))
        h = h8[0:1, :]
        off = pl.multiple_of(c * chunk, chunk)
        act = 0.5 * h * (1.0 + lax.erf(h * 0.7071067811865476))
        a_ref[0, :, pl.ds(off, chunk)] = act * gate_ref[0, :, pl.ds(off, chunk)]
        return carry

    lax.fori_loop(meta_ref[0, 0, k], meta_ref[0, 0, k + 1], chunk_body, 0)


def _peer_acc_kernel(meta_ref, *refs, chunk):
    row_refs = refs[:SUBLANES]
    a_refs = refs[SUBLANES:2 * SUBLANES]
    gtok_ref, res_ref, v_ref, o_ref = refs[2 * SUBLANES:]
    k = pl.program_id(1)
    gpc = chunk // SUBLANES

    @pl.when(k == 0)
    def _():
        o_ref[...] = res_ref[...]

    def groups(gp, carry):
        for sub in range(GROUP_UNROLL):
            g = gp * GROUP_UNROLL + sub
            tok = gtok_ref[0, 0, g]
            alo = ahi = None
            for i in range(SUBLANES):
                r8 = pl.multiple_of(row_refs[i][0, 0, g], SUBLANES)
                lo, hi = _unpack(v_ref[pl.ds(r8, SUBLANES), :])
                a = a_refs[i][0, 0, g]
                alo = a * lo if alo is None else alo + a * lo
                ahi = a * hi if ahi is None else ahi + a * hi
            o_ref[tok, 0:P_ROWS, :] += alo
            o_ref[tok, P_ROWS:D_ROWS, :] += ahi
        return carry

    per_chunk = gpc // GROUP_UNROLL
    lax.fori_loop(meta_ref[0, 0, k] * per_chunk, meta_ref[0, 0, k + 1] * per_chunk, groups, 0)


def pack_table(w):
    e = w.shape[0]
    bits = lax.bitcast_convert_type(w.astype(BF16), jnp.uint16).astype(jnp.uint32)
    half = D_MODEL // 2
    packed = (bits[:, half:] << 16) | bits[:, :half]
    return lax.bitcast_convert_type(packed, jnp.int32).reshape(e * P_ROWS, LANES)


def by_lane(a):
    nb, cap = a.shape[0], a.shape[-1]
    a = a.reshape(nb, cap // SUBLANES, SUBLANES)
    return [a[:, :, i].reshape(nb, 1, cap // SUBLANES) for i in range(SUBLANES)]


def peer_route(eidx, gate, *, n_exp, tile, blk, chunk):
    T, S = eidx.shape
    nb = T // blk
    nt = n_exp // tile
    shift = tile.bit_length() - 1
    assert 1 << shift == tile and T % blk == 0 and S + SUBLANES <= 256 and chunk <= 256
    assert blk & (blk - 1) == 0 and blk < 256 and shift <= ROUTE_POS_SHIFT and nt <= 4
    i32 = jnp.int32
    tiles = jnp.arange(nt, dtype=i32)
    tile_id = eidx >> shift
    cnt = jnp.sum(tile_id[:, :, None] == tiles[None, None, :], axis=1, dtype=i32)
    tpad = (-cnt) % SUBLANES
    t_in = (jnp.arange(T, dtype=i32) % blk)[:, None]
    dead = i32(nt << ROUTE_TILE_SHIFT)
    tile_f, tok_f, pos_f = ROUTE_TILE_SHIFT, ROUTE_TOK_SHIFT, ROUTE_POS_SHIFT
    key_real = ((tile_id << tile_f) | (t_in << tok_f) | (jnp.arange(S, dtype=i32)[None, :] << pos_f)
                | (eidx & (tile - 1)))
    i8 = jnp.arange(SUBLANES, dtype=i32)
    key_tpad = jnp.where(i8[None, None, :] < tpad[:, :, None],
                         (tiles[None, :, None] << tile_f) | (t_in[:, :, None] << tok_f)
                         | ((S + i8)[None, None, :] << pos_f), dead)
    n_run = jnp.sum((cnt + tpad).reshape(nb, blk, nt), axis=1)
    bpad = (-n_run) % chunk
    ib = jnp.arange(chunk - SUBLANES, dtype=i32)
    key_bpad = jnp.where(ib[None, None, :] < bpad[:, :, None],
                         (tiles[None, :, None] << tile_f) | i32(blk << tok_f) | (ib[None, None, :] << pos_f), dead)
    n_real, n_tp, n_bp = blk * S, blk * nt * SUBLANES, nt * (chunk - SUBLANES)
    cap = -(-(n_real + n_tp + nt * chunk) // 1024) * 1024
    fill = cap - (n_real + n_tp + n_bp)
    keys = jnp.concatenate([key_real.reshape(nb, n_real), key_tpad.reshape(nb, n_tp), key_bpad.reshape(nb, n_bp),
                            jnp.full((nb, fill), dead, i32)], axis=1)
    gates = jnp.concatenate([gate.reshape(nb, n_real), jnp.zeros((nb, cap - n_real), gate.dtype)], axis=1)
    keys, gates = lax.sort((keys, gates), dimension=1, num_keys=1)
    rows = (keys & ((1 << pos_f) - 1)) * SUBLANES
    gtok = (keys[:, ::SUBLANES] >> tok_f) & (blk - 1)
    n_chunks = (n_run + bpad) // chunk
    cstart = jnp.concatenate([jnp.zeros((nb, 1), i32), jnp.cumsum(n_chunks, axis=1)], axis=1)
    meta = jnp.pad(cstart, ((0, 0), (0, LANES - nt - 1)))
    r3 = lambda a: a.reshape(nb, 1, a.shape[1])
    return r3(meta), by_lane(rows), r3(gtok), r3(gates), cap


def peer_experts(h, res, eidx, gate, u_packed, v_packed, *, tile, blk):
    T = h.shape[0]
    chunk = PEER_CHUNK_SLOTS
    n_exp = u_packed.shape[0] // P_ROWS
    nt = n_exp // tile
    nb = T // blk
    meta, rows, gtok, gates, cap = peer_route(eidx, gate, n_exp=n_exp, tile=tile, blk=blk, chunk=chunk)
    tok_spec = pl.BlockSpec((blk, D_ROWS, LANES), lambda j, k: (j, 0, 0))
    tab_spec = pl.BlockSpec((tile * P_ROWS, LANES), lambda j, k: (k, 0))
    smem = lambda n: pl.BlockSpec((1, 1, n), lambda j, k: (j, 0, 0), memory_space=pltpu.SMEM,
                                  pipeline_mode=pl.Buffered(1))
    slot_spec = pl.BlockSpec((1, 1, cap), lambda j, k: (j, 0, 0))
    params = pltpu.CompilerParams(dimension_semantics=("parallel", "arbitrary"), vmem_limit_bytes=PEER_VMEM_LIMIT)
    a = pl.pallas_call(
        functools.partial(_peer_dot_kernel, chunk=chunk),
        grid=(nb, nt),
        in_specs=[smem(LANES)] + [smem(cap // SUBLANES)] * (SUBLANES + 1) + [tok_spec, slot_spec, tab_spec],
        out_specs=slot_spec,
        out_shape=jax.ShapeDtypeStruct((nb, 1, cap), F32),
        scratch_shapes=[pltpu.VMEM((chunk, LANES), F32)],
        compiler_params=params, name="peer_dot",
    )(meta, *rows, gtok, h.reshape(T, D_ROWS, LANES), gates, u_packed)
    out = pl.pallas_call(
        functools.partial(_peer_acc_kernel, chunk=chunk),
        grid=(nb, nt),
        in_specs=[smem(LANES)] + [smem(cap // SUBLANES)] * (2 * SUBLANES + 1) + [tok_spec, tab_spec],
        out_specs=tok_spec,
        out_shape=jax.ShapeDtypeStruct((T, D_ROWS, LANES), F32),
        compiler_params=params, name="peer_acc",
    )(meta, *rows, *by_lane(a.reshape(nb, cap)), gtok, res.reshape(T, D_ROWS, LANES), v_packed)
    return out.reshape(T, D_MODEL)


ATT_W = 3 * N_DIL * D_ATT
ROPE_HALF = ROPE_DIM // 2
ATT_TQ = 128
N_QK_BLOCKS = 2 * N_DIL


def rope_lane_tables(length):
    inv = ROPE_THETA ** (-jnp.arange(0, ROPE_DIM, 2, dtype=F32) / ROPE_DIM)
    ang = jnp.arange(length, dtype=F32)[:, None] * inv[None, :]
    cos, sin = jnp.cos(ang), jnp.sin(ang)
    lane = np.arange(LANES) % HEAD_DIM
    j = lane % ROPE_HALF
    first, second = lane < ROPE_HALF, (lane >= ROPE_HALF) & (lane < ROPE_DIM)
    c = jnp.where((first | second)[None, :], cos[:, j], 1.0)
    s1 = jnp.where(second[None, :], sin[:, j], 0.0)
    s2 = jnp.where(first[None, :], -sin[:, j], 0.0)
    return c, s1, s2


def _rope_cast_kernel(x_ref, c_ref, s1_ref, s2_ref, o_ref, *, q_scale):
    col = pl.program_id(1)

    @pl.when(col < N_QK_BLOCKS)
    def _():
        scale = jnp.where(col < N_DIL, q_scale, 1.0)
        c, s1, s2 = c_ref[...], s1_ref[...], s2_ref[...]
        for t in range(D_ATT // LANES):
            x = x_ref[:, t * LANES:(t + 1) * LANES]
            y = x * c + pltpu.roll(x, ROPE_HALF, axis=1) * s1 + pltpu.roll(x, LANES - ROPE_HALF, axis=1) * s2
            o_ref[:, t * LANES:(t + 1) * LANES] = (y * scale).astype(BF16)

    @pl.when(col >= N_QK_BLOCKS)
    def _():
        o_ref[...] = x_ref[...].astype(BF16)


def rope_cast(att, seq_len, *, tm=512):
    T = att.shape[0]
    assert seq_len % tm == 0
    c, s1, s2 = rope_lane_tables(seq_len)
    nblk = seq_len // tm
    tab = pl.BlockSpec((tm, LANES), lambda i, j: (i % nblk, 0))
    blk = pl.BlockSpec((tm, D_ATT), lambda i, j: (i, j))
    return pl.pallas_call(
        functools.partial(_rope_cast_kernel, q_scale=HEAD_DIM ** -0.5),
        grid=(T // tm, ATT_W // D_ATT),
        in_specs=[blk, tab, tab, tab],
        out_specs=blk,
        out_shape=jax.ShapeDtypeStruct((T, ATT_W), BF16),
        compiler_params=pltpu.CompilerParams(dimension_semantics=("parallel", "parallel")),
        name="rope_cast",
    )(att, c, s1, s2)


def _band_attn_kernel(q_ref, kp_ref, kc_ref, kn_ref, vp_ref, vc_ref, vn_ref, o_ref, lse_ref, *, hw, sub_len):
    i = pl.program_id(2)
    tq = q_ref.shape[0]
    q_pos = i * tq + lax.broadcasted_iota(jnp.int32, (tq, 3 * tq), 0)
    k_pos = (i - 1) * tq + lax.broadcasted_iota(jnp.int32, (tq, 3 * tq), 1)
    valid = (jnp.abs(k_pos - q_pos) <= hw) & (k_pos >= 0) & (k_pos < sub_len)
    low = lax.broadcasted_iota(jnp.int32, (tq, LANES), 1) < HEAD_DIM
    contract_last = (((1,), (1,)), ((), ()))
    k_refs = (kp_ref, kc_ref, kn_ref)
    v_refs = (vp_ref, vc_ref, vn_ref)
    for pair in range(D_ATT // LANES):
        sl = slice(pair * LANES, (pair + 1) * LANES)
        q2 = q_ref[:, sl]
        ks = [r[:, sl] for r in k_refs]
        vs = [r[:, sl] for r in v_refs]
        out = jnp.zeros((tq, LANES), F32)
        lse2 = jnp.zeros((tq, LANES), F32)
        for head_lanes in (low, ~low):
            qh = jnp.where(head_lanes, q2, jnp.zeros_like(q2))
            s = jnp.concatenate([lax.dot_general(qh, kj, contract_last, preferred_element_type=F32) for kj in ks],
                                axis=1)
            s = jnp.where(valid, s, NEG)
            m = jnp.max(s, axis=-1, keepdims=True)
            p = jnp.exp(s - m)
            l = jnp.sum(p, axis=-1, keepdims=True)
            pb = p.astype(BF16)
            o = sum(jnp.dot(pb[:, j * tq:(j + 1) * tq], vs[j], preferred_element_type=F32) for j in range(3))
            out = jnp.where(head_lanes, o / l, out)
            lse2 = jnp.where(head_lanes, m + jnp.log(l), lse2)
        o_ref[:, sl] = out
        lse_ref[:, sl] = lse2


def band_attention(qkv, g, batch, seq_len, *, tq=ATT_TQ):
    win, dil = DILATION_CFG[g]
    hw = win // (2 * dil)
    sub_len = seq_len // dil
    assert sub_len % tq == 0 and hw <= tq
    nq = sub_len // tq
    ncol = ATT_W // D_ATT
    x = qkv.reshape(batch, sub_len, dil * ATT_W)

    def spec(kind, shift):
        def index(b, r, i):
            return (b, jnp.clip(i + shift, 0, nq - 1), r * ncol + kind * N_DIL + g)
        return pl.BlockSpec((None, tq, D_ATT), index)

    out_spec = pl.BlockSpec((None, tq, D_ATT), lambda b, r, i: (b, i, r))
    o, lse = pl.pallas_call(
        functools.partial(_band_attn_kernel, hw=hw, sub_len=sub_len),
        grid=(batch, dil, nq),
        in_specs=[spec(0, 0), spec(1, -1), spec(1, 0), spec(1, 1), spec(2, -1), spec(2, 0), spec(2, 1)],
        out_specs=[out_spec, out_spec],
        out_shape=[jax.ShapeDtypeStruct((batch, sub_len, dil * D_ATT), F32)] * 2,
        compiler_params=pltpu.CompilerParams(dimension_semantics=("parallel", "parallel", "parallel"),
                                             vmem_limit_bytes=VMEM_LIMIT_BYTES),
        name=f"band_attn_d{dil}",
    )(x, x, x, x, x, x, x)
    return o.reshape(batch * seq_len, D_ATT), lse.reshape(batch * seq_len, D_ATT)


def _attn_merge_kernel(o0, o1, o2, l0, l1, l2, g_ref, y_ref):
    ls = [l0[...], l1[...], l2[...]]
    m = jnp.maximum(jnp.maximum(ls[0], ls[1]), ls[2])
    es = [jnp.exp(l - m) for l in ls]
    den = es[0] + es[1] + es[2]
    o = (es[0] * o0[...] + es[1] * o1[...] + es[2] * o2[...]) / den
    ms = jnp.mean(o * o, axis=-1, keepdims=True)
    y_ref[...] = o * lax.rsqrt(ms + EPS) * g_ref[...]


def attn_merge(outs, lses, norm_w, *, tm=512):
    T = outs[0].shape[0]
    blk = pl.BlockSpec((tm, D_ATT), lambda i: (i, 0))
    return pl.pallas_call(
        _attn_merge_kernel,
        grid=(T // tm,),
        in_specs=[blk] * 6 + [pl.BlockSpec((1, D_ATT), lambda i: (0, 0))],
        out_specs=blk,
        out_shape=jax.ShapeDtypeStruct((T, D_ATT), F32),
        compiler_params=pltpu.CompilerParams(dimension_semantics=("parallel",)),
        name="attn_merge",
    )(*outs, *lses, norm_w.reshape(1, D_ATT))


def attention_mixer(att, norm_w, batch, seq_len):
    qkv = rope_cast(att, seq_len)
    res = [band_attention(qkv, g, batch, seq_len) for g in range(N_DIL)]
    return attn_merge([r[0] for r in res], [r[1] for r in res], norm_w)


def rmsnorm(x, g):
    xf = x.astype(F32)
    y = xf * lax.rsqrt(jnp.mean(xf * xf, axis=-1, keepdims=True) + EPS) * g.astype(F32)
    return y.astype(x.dtype)


def dwconv(x, w, pad):
    c = x.shape[-1]
    return lax.conv_general_dilated(x, w[:, None, :].astype(x.dtype), window_strides=(1,),
                                    padding=[(pad, pad)], dimension_numbers=('NWC', 'WIO', 'NWC'),
                                    feature_group_count=c)


def ssd_chunked(x, a, bm, cm):
    b, L, H, P = x.shape
    G, N = bm.shape[2], bm.shape[3]
    R = H // G
    Q = SSD_CHUNK
    c = L // Q
    x = x.reshape(b, c, Q, G, R, P)
    a = a.reshape(b, c, Q, G, R)
    bm = bm.reshape(b, c, Q, G, N)
    cm = cm.reshape(b, c, Q, G, N)
    acum = jnp.cumsum(a, axis=2)
    lower = jnp.tril(jnp.ones((Q, Q), dtype=bool))
    seg = acum[:, :, :, None] - acum[:, :, None, :]
    decay = jnp.exp(jnp.where(lower[None, None, :, :, None, None], seg, -jnp.inf))
    cb = jnp.einsum('bclgn,bcsgn->bclsg', cm, bm)
    y_diag = jnp.einsum('bclsgr,bcsgrp->bclgrp', cb[..., None] * decay, x)
    decay_end = jnp.exp(acum[:, :, -1:] - acum)
    states = jnp.einsum('bclgn,bclgrp->bcgrpn', bm, x * decay_end[..., None])
    chunk_decay = jnp.exp(acum[:, :, -1])

    def step(h, inp):
        dec, st = inp
        return dec[..., None, None] * h + st, h

    h0 = jnp.zeros((b, G, R, P, N), x.dtype)
    _, h_in = lax.scan(step, h0, (jnp.moveaxis(chunk_decay, 1, 0), jnp.moveaxis(states, 1, 0)))
    h_in = jnp.moveaxis(h_in, 0, 1)
    y_off = jnp.einsum('bclgn,bcgrpn->bclgrp', cm, h_in) * jnp.exp(acum)[..., None]
    return (y_diag + y_off).reshape(b, L, H, P)


def ssd_mixer(z, xbc, dt_raw, conv_w, conv_b, dt_bias, a_log, d_skip, norm_w):
    b, L, _ = z.shape
    xbc = jax.nn.silu(dwconv(xbc, conv_w, SSD_CONV // 2) + conv_b.astype(xbc.dtype))
    xs = xbc[..., :D_SSD].reshape(b, L, SSD_HEADS, HEAD_DIM).astype(F32)
    bm = xbc[..., D_SSD:D_SSD + SSD_GROUPS * SSD_STATE].reshape(b, L, SSD_GROUPS, SSD_STATE).astype(F32)
    cm = xbc[..., D_SSD + SSD_GROUPS * SSD_STATE:].reshape(b, L, SSD_GROUPS, SSD_STATE).astype(F32)
    dt = jax.nn.softplus(dt_raw.astype(F32).reshape(b, L, 2, SSD_HEADS) + dt_bias.astype(F32))
    A = -jnp.exp(a_log.astype(F32))
    y_f = ssd_chunked(xs * dt[:, :, 0, :, None], dt[:, :, 0] * A[0], bm, cm)
    fl = lambda t: jnp.flip(t, axis=1)
    y_b = fl(ssd_chunked(fl(xs * dt[:, :, 1, :, None]), fl(dt[:, :, 1] * A[1]), fl(bm), fl(cm)))
    y = y_f + y_b + xs * d_skip.astype(F32)[:, None]
    y = y.reshape(b, L, D_SSD) * jax.nn.silu(z.astype(F32))
    yg = y.reshape(b, L, SSD_GROUPS, D_SSD // SSD_GROUPS)
    yg = yg * lax.rsqrt(jnp.mean(yg * yg, axis=-1, keepdims=True) + EPS)
    return (yg.reshape(b, L, D_SSD) * norm_w.astype(F32)).astype(z.dtype)


def shortconv_mixer(sc, conv_w, norm_w):
    bg, cg, hx = jnp.split(sc, 3, axis=-1)
    y = bg * dwconv(cg * hx, conv_w, SC_CONV // 2)
    return rmsnorm(y, norm_w)


def trunk(x, norm_mix, w_in, ssd_conv_w, ssd_conv_b, ssd_dt_bias, ssd_a_log, ssd_d, ssd_norm,
          sc_conv_w, sc_norm, att_norm, w_out, norm_ffn, peer_wq, peer_subkeys, peer_u, peer_v, norm_final):
    b, L, D = x.shape
    T = b * L
    x = x.reshape(T, D)
    o_xbc = D_SSD
    o_dt = o_xbc + XBC
    o_sc = o_dt + 2 * SSD_HEADS
    o_att = o_sc + 3 * D_SC
    for l in range(DEPTH):
        w = w_in[l]
        g = norm_mix[l]
        w_dt = jnp.pad(w[:, o_dt:o_sc], ((0, 0), (0, DT_PAD - 2 * SSD_HEADS)))
        z = norm_matmul(x, g, w[:, :o_xbc].astype(BF16))
        xbc = norm_matmul(x, g, w[:, o_xbc:o_dt].astype(BF16))
        dt_raw = norm_matmul(x, g, w_dt.astype(BF16))[:, :2 * SSD_HEADS]
        sc = norm_matmul(x, g, w[:, o_sc:o_att].astype(BF16), tn=768)
        att = norm_matmul(x, g, w[:, o_att:].astype(BF16), tn=768)
        r3 = lambda t: t.reshape(b, L, t.shape[-1])
        y = jnp.concatenate([
            ssd_mixer(r3(z), r3(xbc), r3(dt_raw), ssd_conv_w[l], ssd_conv_b[l], ssd_dt_bias[l],
                      ssd_a_log[l], ssd_d[l], ssd_norm[l]).reshape(T, D_SSD),
            shortconv_mixer(r3(sc), sc_conv_w[l], sc_norm[l]).reshape(T, D_SC),
            attention_mixer(att, att_norm[l], b, L)], axis=-1)
        x = matmul_residual(y, w_out[l].astype(BF16), x)
        q = norm_matmul(x, norm_ffn[l], peer_wq[l].astype(BF16))
        h = rmsnorm_rows(x, norm_ffn[l])
        eidx, gate = peer_topk(q, peer_subkeys[l])
        x = peer_experts(h, x, eidx, gate, pack_table(peer_u[l]), pack_table(peer_v[l]),
                         tile=PEER_TILE, blk=PEER_BLK)
    return rmsnorm_rows(x, norm_final).reshape(b, L, D)


def kernel(x_prompt, x_sample, norm_mix, w_in, ssd_conv_w, ssd_conv_b, ssd_dt_bias, ssd_a_log, ssd_d, ssd_norm,
           sc_conv_w, sc_norm, att_norm, w_out, norm_ffn, peer_wq, peer_subkeys, peer_u, peer_v, norm_final):
    nb = x_prompt.shape[0]
    x = jnp.concatenate([x_prompt, x_sample], axis=0)
    y = trunk(x, norm_mix, w_in, ssd_conv_w, ssd_conv_b, ssd_dt_bias, ssd_a_log, ssd_d, ssd_norm,
              sc_conv_w, sc_norm, att_norm, w_out, norm_ffn, peer_wq, peer_subkeys, peer_u, peer_v, norm_final)
    return (y[:nb], y[nb:])
```

```python
import functools

import jax
import jax.numpy as jnp
import numpy as np
from jax import lax
from jax.experimental import pallas as pl
from jax.experimental.pallas import tpu as pltpu

D_MODEL = 2048
DEPTH = 2
D_MIX = 3 * D_MODEL // 2
HEAD_DIM = 64
D_SSD = D_MIX // 2
SSD_HEADS = D_SSD // HEAD_DIM
SSD_GROUPS = 4
SSD_STATE = 128
SSD_CONV = 5
SSD_CHUNK = 128
XBC = D_SSD + 2 * SSD_GROUPS * SSD_STATE
D_SC = D_MIX // 4
SC_CONV = 3
D_ATT = D_MIX // 4
DILATION_CFG = ((128, 1), (512, 4), (2048, 16))
N_DIL = len(DILATION_CFG)
ROPE_DIM = HEAD_DIM // 4
ROPE_THETA = 500000.0
PEER_HEADS = 8
PEER_NKEYS = 128
PEER_TOPK = 16
PEER_DQ = 256
EPS = 1e-6
NEG = -1e30
DT_PAD = 128

LANES = 128
SUBLANES = 8
D_ROWS = D_MODEL // LANES
P_ROWS = D_ROWS // 2
PEER_TILE = 4096
PEER_BLK = 128
PEER_CHUNK_SLOTS = 256
GROUP_UNROLL = 4
ROUTE_POS_SHIFT = 12
ROUTE_TOK_SHIFT = 20
ROUTE_TILE_SHIFT = 28

VMEM_LIMIT_BYTES = 56 * 1024 * 1024
PEER_VMEM_LIMIT = VMEM_LIMIT_BYTES

F32 = jnp.float32
BF16 = jnp.bfloat16


def _norm_matmul_kernel(x_ref, g_ref, w_ref, o_ref, h_scr):
    @pl.when(pl.program_id(1) == 0)
    def _():
        x = x_ref[...]
        ms = jnp.mean(x * x, axis=-1, keepdims=True)
        h_scr[...] = (x * lax.rsqrt(ms + EPS) * g_ref[...]).astype(BF16)

    o_ref[...] = jnp.dot(h_scr[...], w_ref[...], preferred_element_type=F32)


def norm_matmul(x, g, w, *, tm=512, tn=512):
    m, k = x.shape
    n = w.shape[1]
    tn = min(tn, n)
    assert m % tm == 0 and n % tn == 0
    return pl.pallas_call(
        _norm_matmul_kernel,
        grid=(m // tm, n // tn),
        in_specs=[
            pl.BlockSpec((tm, k), lambda i, j: (i, 0)),
            pl.BlockSpec((1, k), lambda i, j: (0, 0)),
            pl.BlockSpec((k, tn), lambda i, j: (0, j)),
        ],
        out_specs=pl.BlockSpec((tm, tn), lambda i, j: (i, j)),
        out_shape=jax.ShapeDtypeStruct((m, n), F32),
        scratch_shapes=[pltpu.VMEM((tm, k), BF16)],
        compiler_params=pltpu.CompilerParams(
            dimension_semantics=("parallel", "arbitrary"),
            vmem_limit_bytes=VMEM_LIMIT_BYTES),
        name="norm_matmul",
    )(x, g.reshape(1, k), w)


def _matmul_residual_kernel(y_ref, w_ref, r_ref, o_ref):
    o_ref[...] = r_ref[...] + jnp.dot(y_ref[...].astype(BF16), w_ref[...],
                                      preferred_element_type=F32)


def matmul_residual(y, w, res, *, tm=512, tn=512):
    m, k = y.shape
    n = w.shape[1]
    assert m % tm == 0 and n % tn == 0
    return pl.pallas_call(
        _matmul_residual_kernel,
        grid=(m // tm, n // tn),
        in_specs=[
            pl.BlockSpec((tm, k), lambda i, j: (i, 0)),
            pl.BlockSpec((k, tn), lambda i, j: (0, j)),
            pl.BlockSpec((tm, tn), lambda i, j: (i, j)),
        ],
        out_specs=pl.BlockSpec((tm, tn), lambda i, j: (i, j)),
        out_shape=jax.ShapeDtypeStruct((m, n), F32),
        compiler_params=pltpu.CompilerParams(
            dimension_semantics=("parallel", "arbitrary"),
            vmem_limit_bytes=VMEM_LIMIT_BYTES),
        name="matmul_residual",
    )(y, w, res)


def _rmsnorm_kernel(x_ref, g_ref, o_ref):
    x = x_ref[...]
    ms = jnp.mean(x * x, axis=-1, keepdims=True)
    o_ref[...] = x * lax.rsqrt(ms + EPS) * g_ref[...]


def rmsnorm_rows(x, g, *, tm=512):
    m, k = x.shape
    return pl.pallas_call(
        _rmsnorm_kernel,
        grid=(m // tm,),
        in_specs=[pl.BlockSpec((tm, k), lambda i: (i, 0)),
                  pl.BlockSpec((1, k), lambda i: (0, 0))],
        out_specs=pl.BlockSpec((tm, k), lambda i: (i, 0)),
        out_shape=jax.ShapeDtypeStruct((m, k), F32),
        compiler_params=pltpu.CompilerParams(dimension_semantics=("parallel",)),
        name="rmsnorm",
    )(x, g.reshape(1, k))


NCAND = PEER_TOPK * PEER_TOPK
ENC_SHIFT = PEER_NKEYS * PEER_NKEYS
CAND_AB = [(a, b) for a in range(PEER_TOPK) for b in range(PEER_TOPK) if (a + 1) * (b + 1) <= PEER_TOPK]
NCAND_KEPT = -(-len(CAND_AB) // SUBLANES) * SUBLANES
NSEL_EXT = 2 * PEER_TOPK + SUBLANES
NEG_INF = float("-inf")


def candidate_matrices():
    expand = np.zeros((NCAND_KEPT, NSEL_EXT), np.float32)
    scale = np.zeros((NCAND_KEPT, NSEL_EXT), np.float32)
    for j, (a, b) in enumerate(CAND_AB):
        expand[j, a] = expand[j, PEER_TOPK + b] = 1.0
        scale[j, a], scale[j, PEER_TOPK + b] = PEER_NKEYS, 1.0
        scale[j, 2 * PEER_TOPK] = (a * PEER_TOPK + b) * ENC_SHIFT
    expand[len(CAND_AB):, 2 * PEER_TOPK] = -1e30
    scale[len(CAND_AB):, 2 * PEER_TOPK] = (NCAND - 1) * ENC_SHIFT
    return jnp.asarray(expand, BF16), jnp.asarray(scale, BF16)


def _split3(x):
    p0 = x.astype(BF16)
    r = x - p0.astype(F32)
    p1 = r.astype(BF16)
    p2 = (r - p1.astype(F32)).astype(BF16)
    return p0, p1, p2


def _peer_topk_kernel(q_ref, k_ref, expand_ref, scale_ref, eidx_ref, gate_ref,
                      s_scr, sv_scr, si_scr, c_scr, enc_scr, top_scr, e_scr):
    tm = q_ref.shape[0]
    half = PEER_DQ // 2
    key_f = lax.broadcasted_iota(jnp.int32, (PEER_NKEYS, tm), 0).astype(F32)
    contract_last = (((1,), (1,)), ((), ()))

    for i in range(2):
        q = q_ref[:, i * half:(i + 1) * half].astype(BF16)
        s_scr[i] = lax.dot_general(k_ref[0, i], q, contract_last, preferred_element_type=F32)

    def round1(r, carry):
        for i in range(2):
            s = s_scr[i]
            m = jnp.max(s, axis=0, keepdims=True)
            pos = jnp.min(jnp.where(s == m, key_f, float(PEER_NKEYS)), axis=0, keepdims=True)
            s_scr[i] = jnp.where(key_f == pos, NEG_INF, s)
            sv_scr[pl.ds(i * PEER_TOPK + r, 1), :] = m
            si_scr[pl.ds(i * PEER_TOPK + r, 1), :] = pos
        return carry

    lax.fori_loop(0, PEER_TOPK, round1, 0)

    ones_row = (lax.broadcasted_iota(jnp.int32, (NSEL_EXT - 2 * PEER_TOPK, tm), 0) == 0).astype(F32)
    sv_scr[2 * PEER_TOPK:, :] = ones_row
    si_scr[2 * PEER_TOPK:, :] = ones_row
    c_scr[...] = sum(jnp.dot(expand_ref[...], p, preferred_element_type=F32) for p in _split3(sv_scr[...]))
    enc_scr[...] = jnp.dot(scale_ref[...], si_scr[...].astype(BF16), preferred_element_type=F32)
    big = float(NCAND * ENC_SHIFT)

    def round2(r, carry):
        c = c_scr[...]
        enc = enc_scr[...]
        m = jnp.max(c, axis=0, keepdims=True)
        e = jnp.min(jnp.where(c == m, enc, big), axis=0, keepdims=True)
        c_scr[...] = jnp.where(enc == e, NEG_INF, c)
        top_scr[pl.ds(r, 1), :] = m
        e_scr[pl.ds(r, 1), :] = e
        return carry

    lax.fori_loop(0, PEER_TOPK, round2, 0)
    top = top_scr[...]
    ex = jnp.exp(top - jnp.max(top, axis=0, keepdims=True))
    gate_ref[0] = ex / jnp.sum(ex, axis=0, keepdims=True)
    eidx_ref[0] = e_scr[...].astype(jnp.int32) & (ENC_SHIFT - 1)


def peer_topk(q, subkeys, *, tm=512):
    T = q.shape[0]
    assert T % tm == 0
    vm = pltpu.VMEM
    eidx, gate = pl.pallas_call(
        _peer_topk_kernel,
        grid=(T // tm, PEER_HEADS),
        in_specs=[pl.BlockSpec((tm, PEER_DQ), lambda i, h: (i, h)),
                  pl.BlockSpec((1, 2, PEER_NKEYS, PEER_DQ // 2), lambda i, h: (h, 0, 0, 0)),
                  pl.BlockSpec((NCAND_KEPT, NSEL_EXT), lambda i, h: (0, 0)),
                  pl.BlockSpec((NCAND_KEPT, NSEL_EXT), lambda i, h: (0, 0))],
        out_specs=[pl.BlockSpec((1, PEER_TOPK, tm), lambda i, h: (h, 0, i)),
                   pl.BlockSpec((1, PEER_TOPK, tm), lambda i, h: (h, 0, i))],
        out_shape=[jax.ShapeDtypeStruct((PEER_HEADS, PEER_TOPK, T), jnp.int32),
                   jax.ShapeDtypeStruct((PEER_HEADS, PEER_TOPK, T), F32)],
        scratch_shapes=[vm((2, PEER_NKEYS, tm), F32), vm((NSEL_EXT, tm), F32), vm((NSEL_EXT, tm), F32),
                        vm((NCAND_KEPT, tm), F32), vm((NCAND_KEPT, tm), F32), vm((PEER_TOPK, tm), F32),
                        vm((PEER_TOPK, tm), F32)],
        compiler_params=pltpu.CompilerParams(dimension_semantics=("parallel", "arbitrary")),
        name="peer_topk",
    )(q, subkeys.astype(BF16), *candidate_matrices())
    fix = lambda a: a.transpose(2, 0, 1).reshape(T, PEER_HEADS * PEER_TOPK)
    return fix(eidx), fix(gate)


def _fold(xa, xb, mask, shift):
    s = jnp.where(mask, xa, xb)
    t = jnp.where(mask, xb, xa)
    return s + pltpu.roll(t, shift, axis=0)


def _unpack(w):
    lo = lax.bitcast_convert_type(w << 16, F32)
    hi = lax.bitcast_convert_type(w & jnp.int32(-65536), F32)
    return lo, hi


def _peer_dot_kernel(meta_ref, *refs, chunk):
    row_refs = refs[:SUBLANES]
    gtok_ref, x_ref, gate_ref, u_ref, a_ref, c_scr = refs[SUBLANES:]
    k = pl.program_id(1)
    gpc = chunk // SUBLANES

    @pl.when(k == 0)
    def _():
        a_ref[...] = jnp.zeros_like(a_ref)

    row_i = lax.broadcasted_iota(jnp.int32, (SUBLANES, LANES), 0)
    m1 = (row_i & 1) == 0
    m2 = (row_i & 2) == 0
    m3 = (row_i & 4) == 0
    ones = jnp.ones((SUBLANES, LANES), BF16)

    def chunk_body(c, carry):
        def groups(gi, carry):
            for sub in range(GROUP_UNROLL):
                gl = gi * GROUP_UNROLL + sub
                g = c * gpc + gl
                tok = gtok_ref[0, 0, g]
                xlo = x_ref[tok, 0:P_ROWS, :]
                xhi = x_ref[tok, P_ROWS:D_ROWS, :]
                ps = []
                for i in range(SUBLANES):
                    r8 = pl.multiple_of(row_refs[i][0, 0, g], SUBLANES)
                    lo, hi = _unpack(u_ref[pl.ds(r8, SUBLANES), :])
                    ps.append(lo * xlo + hi * xhi)
                q = [_fold(ps[2 * i], ps[2 * i + 1], m1, 1) for i in range(4)]
                r = [_fold(q[0], q[1], m2, 2), _fold(q[2], q[3], m2, 2)]
                c_scr[pl.ds(pl.multiple_of(gl * SUBLANES, SUBLANES), SUBLANES), :] = _fold(r[0], r[1], m3, 4)
            return carry

        lax.fori_loop(0, gpc // GROUP_UNROLL, groups, 0)
        dims = (((1,), (1,)), ((), ()))
        h8 = sum(lax.dot_general(ones, p, dims, preferred_element_type=F32) for p in _split3(c_scr[...]))
        h = h8[0:1, :]
        off = pl.multiple_of(c * chunk, chunk)
        act = 0.5 * h * (1.0 + lax.erf(h * 0.7071067811865476))
        a_ref[0, :, pl.ds(off, chunk)] = act * gate_ref[0, :, pl.ds(off, chunk)]
        return carry

    lax.fori_loop(meta_ref[0, 0, k], meta_ref[0, 0, k + 1], chunk_body, 0)


def _peer_acc_kernel(meta_ref, *refs, chunk):
    row_refs = refs[:SUBLANES]
    a_refs = refs[SUBLANES:2 * SUBLANES]
    gtok_ref, res_ref, v_ref, o_ref = refs[2 * SUBLANES:]
    k = pl.program_id(1)
    gpc = chunk // SUBLANES

    @pl.when(k == 0)
    def _():
        o_ref[...] = res_ref[...]

    def groups(gp, carry):
        for sub in range(GROUP_UNROLL):
            g = gp * GROUP_UNROLL + sub
            tok = gtok_ref[0, 0, g]
            alo = ahi = None
            for i in range(SUBLANES):
                r8 = pl.multiple_of(row_refs[i][0, 0, g], SUBLANES)
                lo, hi = _unpack(v_ref[pl.ds(r8, SUBLANES), :])
                a = a_refs[i][0, 0, g]
                alo = a * lo if alo is None else alo + a * lo
                ahi = a * hi if ahi is None else ahi + a * hi
            o_ref[tok, 0:P_ROWS, :] += alo
            o_ref[tok, P_ROWS:D_ROWS, :] += ahi
        return carry

    per_chunk = gpc // GROUP_UNROLL
    lax.fori_loop(meta_ref[0, 0, k] * per_chunk, meta_ref[0, 0, k + 1] * per_chunk, groups, 0)


def pack_table(w):
    e = w.shape[0]
    bits = lax.bitcast_convert_type(w.astype(BF16), jnp.uint16).astype(jnp.uint32)
    half = D_MODEL // 2
    packed = (bits[:, half:] << 16) | bits[:, :half]
    return lax.bitcast_convert_type(packed, jnp.int32).reshape(e * P_ROWS, LANES)


def by_lane(a):
    nb, cap = a.shape[0], a.shape[-1]
    a = a.reshape(nb, cap // SUBLANES, SUBLANES)
    return [a[:, :, i].reshape(nb, 1, cap // SUBLANES) for i in range(SUBLANES)]


def peer_route(eidx, gate, *, n_exp, tile, blk, chunk):
    T, S = eidx.shape
    nb = T // blk
    nt = n_exp // tile
    shift = tile.bit_length() - 1
    assert 1 << shift == tile and T % blk == 0 and S + SUBLANES <= 256 and chunk <= 256
    assert blk & (blk - 1) == 0 and blk < 256 and shift <= ROUTE_POS_SHIFT and nt <= 4
    i32 = jnp.int32
    tiles = jnp.arange(nt, dtype=i32)
    tile_id = eidx >> shift
    cnt = jnp.sum(tile_id[:, :, None] == tiles[None, None, :], axis=1, dtype=i32)
    tpad = (-cnt) % SUBLANES
    t_in = (jnp.arange(T, dtype=i32) % blk)[:, None]
    dead = i32(nt << ROUTE_TILE_SHIFT)
    tile_f, tok_f, pos_f = ROUTE_TILE_SHIFT, ROUTE_TOK_SHIFT, ROUTE_POS_SHIFT
    key_real = ((tile_id << tile_f) | (t_in << tok_f) | (jnp.arange(S, dtype=i32)[None, :] << pos_f)
                | (eidx & (tile - 1)))
    i8 = jnp.arange(SUBLANES, dtype=i32)
    key_tpad = jnp.where(i8[None, None, :] < tpad[:, :, None],
                         (tiles[None, :, None] << tile_f) | (t_in[:, :, None] << tok_f)
                         | ((S + i8)[None, None, :] << pos_f), dead)
    n_run = jnp.sum((cnt + tpad).reshape(nb, blk, nt), axis=1)
    bpad = (-n_run) % chunk
    ib = jnp.arange(chunk - SUBLANES, dtype=i32)
    key_bpad = jnp.where(ib[None, None, :] < bpad[:, :, None],
                         (tiles[None, :, None] << tile_f) | i32(blk << tok_f) | (ib[None, None, :] << pos_f), dead)
    n_real, n_tp, n_bp = blk * S, blk * nt * SUBLANES, nt * (chunk - SUBLANES)
    cap = -(-(n_real + n_tp + nt * chunk) // 1024) * 1024
    fill = cap - (n_real + n_tp + n_bp)
    keys = jnp.concatenate([key_real.reshape(nb, n_real), key_tpad.reshape(nb, n_tp), key_bpad.reshape(nb, n_bp),
                            jnp.full((nb, fill), dead, i32)], axis=1)
    gates = jnp.concatenate([gate.reshape(nb, n_real), jnp.zeros((nb, cap - n_real), gate.dtype)], axis=1)
    keys, gates = lax.sort((keys, gates), dimension=1, num_keys=1)
    rows = (keys & ((1 << pos_f) - 1)) * SUBLANES
    gtok = (keys[:, ::SUBLANES] >> tok_f) & (blk - 1)
    n_chunks = (n_run + bpad) // chunk
    cstart = jnp.concatenate([jnp.zeros((nb, 1), i32), jnp.cumsum(n_chunks, axis=1)], axis=1)
    meta = jnp.pad(cstart, ((0, 0), (0, LANES - nt - 1)))
    r3 = lambda a: a.reshape(nb, 1, a.shape[1])
    return r3(meta), by_lane(rows), r3(gtok), r3(gates), cap


def peer_experts(h, res, eidx, gate, u_packed, v_packed, *, tile, blk):
    T = h.shape[0]
    chunk = PEER_CHUNK_SLOTS
    n_exp = u_packed.shape[0] // P_ROWS
    nt = n_exp // tile
    nb = T // blk
    meta, rows, gtok, gates, cap = peer_route(eidx, gate, n_exp=n_exp, tile=tile, blk=blk, chunk=chunk)
    tok_spec = pl.BlockSpec((blk, D_ROWS, LANES), lambda j, k: (j, 0, 0))
    tab_spec = pl.BlockSpec((tile * P_ROWS, LANES), lambda j, k: (k, 0))
    smem = lambda n: pl.BlockSpec((1, 1, n), lambda j, k: (j, 0, 0), memory_space=pltpu.SMEM,
                                  pipeline_mode=pl.Buffered(1))
    slot_spec = pl.BlockSpec((1, 1, cap), lambda j, k: (j, 0, 0))
    params = pltpu.CompilerParams(dimension_semantics=("parallel", "arbitrary"), vmem_limit_bytes=PEER_VMEM_LIMIT)
    a = pl.pallas_call(
        functools.partial(_peer_dot_kernel, chunk=chunk),
        grid=(nb, nt),
        in_specs=[smem(LANES)] + [smem(cap // SUBLANES)] * (SUBLANES + 1) + [tok_spec, slot_spec, tab_spec],
        out_specs=slot_spec,
        out_shape=jax.ShapeDtypeStruct((nb, 1, cap), F32),
        scratch_shapes=[pltpu.VMEM((chunk, LANES), F32)],
        compiler_params=params, name="peer_dot",
    )(meta, *rows, gtok, h.reshape(T, D_ROWS, LANES), gates, u_packed)
    out = pl.pallas_call(
        functools.partial(_peer_acc_kernel, chunk=chunk),
        grid=(nb, nt),
        in_specs=[smem(LANES)] + [smem(cap // SUBLANES)] * (2 * SUBLANES + 1) + [tok_spec, tab_spec],
        out_specs=tok_spec,
        out_shape=jax.ShapeDtypeStruct((T, D_ROWS, LANES), F32),
        compiler_params=params, name="peer_acc",
    )(meta, *rows, *by_lane(a.reshape(nb, cap)), gtok, res.reshape(T, D_ROWS, LANES), v_packed)
    return out.reshape(T, D_MODEL)


ATT_W = 3 * N_DIL * D_ATT
ROPE_HALF = ROPE_DIM // 2
ATT_TQ = 128
N_QK_BLOCKS = 2 * N_DIL


def rope_lane_tables(length):
    inv = ROPE_THETA ** (-jnp.arange(0, ROPE_DIM, 2, dtype=F32) / ROPE_DIM)
    ang = jnp.arange(length, dtype=F32)[:, None] * inv[None, :]
    cos, sin = jnp.cos(ang), jnp.sin(ang)
    lane = np.arange(LANES) % HEAD_DIM
    j = lane % ROPE_HALF
    first, second = lane < ROPE_HALF, (lane >= ROPE_HALF) & (lane < ROPE_DIM)
    c = jnp.where((first | second)[None, :], cos[:, j], 1.0)
    s1 = jnp.where(second[None, :], sin[:, j], 0.0)
    s2 = jnp.where(first[None, :], -sin[:, j], 0.0)
    return c, s1, s2


def _rope_cast_kernel(x_ref, c_ref, s1_ref, s2_ref, o_ref, *, q_scale):
    col = pl.program_id(1)

    @pl.when(col < N_QK_BLOCKS)
    def _():
        scale = jnp.where(col < N_DIL, q_scale, 1.0)
        c, s1, s2 = c_ref[...], s1_ref[...], s2_ref[...]
        for t in range(D_ATT // LANES):
            x = x_ref[:, t * LANES:(t + 1) * LANES]
            y = x * c + pltpu.roll(x, ROPE_HALF, axis=1) * s1 + pltpu.roll(x, LANES - ROPE_HALF, axis=1) * s2
            o_ref[:, t * LANES:(t + 1) * LANES] = (y * scale).astype(BF16)

    @pl.when(col >= N_QK_BLOCKS)
    def _():
        o_ref[...] = x_ref[...].astype(BF16)


def rope_cast(att, seq_len, *, tm=512):
    T = att.shape[0]
    assert seq_len % tm == 0
    c, s1, s2 = rope_lane_tables(seq_len)
    nblk = seq_len // tm
    tab = pl.BlockSpec((tm, LANES), lambda i, j: (i % nblk, 0))
    blk = pl.BlockSpec((tm, D_ATT), lambda i, j: (i, j))
    return pl.pallas_call(
        functools.partial(_rope_cast_kernel, q_scale=HEAD_DIM ** -0.5),
        grid=(T // tm, ATT_W // D_ATT),
        in_specs=[blk, tab, tab, tab],
        out_specs=blk,
        out_shape=jax.ShapeDtypeStruct((T, ATT_W), BF16),
        compiler_params=pltpu.CompilerParams(dimension_semantics=("parallel", "parallel")),
        name="rope_cast",
    )(att, c, s1, s2)


def _band_attn_kernel(q_ref, kp_ref, kc_ref, kn_ref, vp_ref, vc_ref, vn_ref, o_ref, lse_ref, *, hw, sub_len):
    i = pl.program_id(2)
    tq = q_ref.shape[0]
    q_pos = i * tq + lax.broadcasted_iota(jnp.int32, (tq, 3 * tq), 0)
    k_pos = (i - 1) * tq + lax.broadcasted_iota(jnp.int32, (tq, 3 * tq), 1)
    valid = (jnp.abs(k_pos - q_pos) <= hw) & (k_pos >= 0) & (k_pos < sub_len)
    low = lax.broadcasted_iota(jnp.int32, (tq, LANES), 1) < HEAD_DIM
    contract_last = (((1,), (1,)), ((), ()))
    k_refs = (kp_ref, kc_ref, kn_ref)
    v_refs = (vp_ref, vc_ref, vn_ref)
    for pair in range(D_ATT // LANES):
        sl = slice(pair * LANES, (pair + 1) * LANES)
        q2 = q_ref[:, sl]
        ks = [r[:, sl] for r in k_refs]
        vs = [r[:, sl] for r in v_refs]
        out = jnp.zeros((tq, LANES), F32)
        lse2 = jnp.zeros((tq, LANES), F32)
        for head_lanes in (low, ~low):
            qh = jnp.where(head_lanes, q2, jnp.zeros_like(q2))
            s = jnp.concatenate([lax.dot_general(qh, kj, contract_last, preferred_element_type=F32) for kj in ks],
                                axis=1)
            s = jnp.where(valid, s, NEG)
            m = jnp.max(s, axis=-1, keepdims=True)
            p = jnp.exp(s - m)
            l = jnp.sum(p, axis=-1, keepdims=True)
            pb = p.astype(BF16)
            o = sum(jnp.dot(pb[:, j * tq:(j + 1) * tq], vs[j], preferred_element_type=F32) for j in range(3))
            out = jnp.where(head_lanes, o / l, out)
            lse2 = jnp.where(head_lanes, m + jnp.log(l), lse2)
        o_ref[:, sl] = out
        lse_ref[:, sl] = lse2


def band_attention(qkv, g, batch, seq_len, *, tq=ATT_TQ):
    win, dil = DILATION_CFG[g]
    hw = win // (2 * dil)
    sub_len = seq_len // dil
    assert sub_len % tq == 0 and hw <= tq
    nq = sub_len // tq
    ncol = ATT_W // D_ATT
    x = qkv.reshape(batch, sub_len, dil * ATT_W)

    def spec(kind, shift):
        def index(b, r, i):
            return (b, jnp.clip(i + shift, 0, nq - 1), r * ncol + kind * N_DIL + g)
        return pl.BlockSpec((None, tq, D_ATT), index)

    out_spec = pl.BlockSpec((None, tq, D_ATT), lambda b, r, i: (b, i, r))
    o, lse = pl.pallas_call(
        functools.partial(_band_attn_kernel, hw=hw, sub_len=sub_len),
        grid=(batch, dil, nq),
        in_specs=[spec(0, 0), spec(1, -1), spec(1, 0), spec(1, 1), spec(2, -1), spec(2, 0), spec(2, 1)],
        out_specs=[out_spec, out_spec],
        out_shape=[jax.ShapeDtypeStruct((batch, sub_len, dil * D_ATT), F32)] * 2,
        compiler_params=pltpu.CompilerParams(dimension_semantics=("parallel", "parallel", "parallel"),
                                             vmem_limit_bytes=VMEM_LIMIT_BYTES),
        name=f"band_attn_d{dil}",
    )(x, x, x, x, x, x, x)
    return o.reshape(batch * seq_len, D_ATT), lse.reshape(batch * seq_len, D_ATT)


def _attn_merge_kernel(o0, o1, o2, l0, l1, l2, g_ref, y_ref):
    ls = [l0[...], l1[...], l2[...]]
    m = jnp.maximum(jnp.maximum(ls[0], ls[1]), ls[2])
    es = [jnp.exp(l - m) for l in ls]
    den = es[0] + es[1] + es[2]
    o = (es[0] * o0[...] + es[1] * o1[...] + es[2] * o2[...]) / den
    ms = jnp.mean(o * o, axis=-1, keepdims=True)
    y_ref[...] = o * lax.rsqrt(ms + EPS) * g_ref[...]


def attn_merge(outs, lses, norm_w, *, tm=512):
    T = outs[0].shape[0]
    blk = pl.BlockSpec((tm, D_ATT), lambda i: (i, 0))
    return pl.pallas_call(
        _attn_merge_kernel,
        grid=(T // tm,),
        in_specs=[blk] * 6 + [pl.BlockSpec((1, D_ATT), lambda i: (0, 0))],
        out_specs=blk,
        out_shape=jax.ShapeDtypeStruct((T, D_ATT), F32),
        compiler_params=pltpu.CompilerParams(dimension_semantics=("parallel",)),
        name="attn_merge",
    )(*outs, *lses, norm_w.reshape(1, D_ATT))


def attention_mixer(att, norm_w, batch, seq_len):
    qkv = rope_cast(att, seq_len)
    res = [band_attention(qkv, g, batch, seq_len) for g in range(N_DIL)]
    return attn_merge([r[0] for r in res], [r[1] for r in res], norm_w)


CONV_HALO = SUBLANES


def _conv_taps(x, w_ref, pad_scr, width):
    n = x.shape[0]
    half = width // 2
    zeros = jnp.zeros((CONV_HALO, LANES), F32)
    pad_scr[0:CONV_HALO, :] = zeros
    pad_scr[CONV_HALO + n:CONV_HALO + n + CONV_HALO, :] = zeros
    pad_scr[CONV_HALO:CONV_HALO + n, :] = x
    acc = jnp.zeros((n, LANES), F32)
    for k in range(width):
        acc = acc + pad_scr[CONV_HALO - half + k:CONV_HALO - half + k + n, :] * w_ref[k:k + 1, :]
    return acc


def _silu(x):
    return x * (1.0 / (1.0 + jnp.exp(-x)))


def _ssd_conv_kernel(x_ref, w_ref, b_ref, o_ref, pad_scr):
    o_ref[...] = _silu(_conv_taps(x_ref[...], w_ref, pad_scr, SSD_CONV) + b_ref[...])


def _shortconv_kernel(bg_ref, cg_ref, hx_ref, w_ref, o_ref, pad_scr):
    o_ref[...] = bg_ref[...] * _conv_taps(cg_ref[...] * hx_ref[...], w_ref, pad_scr, SC_CONV)


def _strip_call(kernel, name, n_out, batch, seq_len, in_specs, args):
    return pl.pallas_call(
        kernel,
        grid=(batch, n_out // LANES),
        in_specs=in_specs,
        out_specs=pl.BlockSpec((seq_len, LANES), lambda b, c: (b, c)),
        out_shape=jax.ShapeDtypeStruct((batch * seq_len, n_out), F32),
        scratch_shapes=[pltpu.VMEM((seq_len + 2 * CONV_HALO, LANES), F32)],
        compiler_params=pltpu.CompilerParams(dimension_semantics=("parallel", "parallel"),
                                             vmem_limit_bytes=VMEM_LIMIT_BYTES),
        name=name,
    )(*args)


def ssd_conv(xbc, w, bias, batch, seq_len):
    strip = lambda off: pl.BlockSpec((seq_len, LANES), lambda b, c: (b, c + off))
    taps = pl.BlockSpec((SSD_CONV, LANES), lambda b, c: (0, c))
    vec = pl.BlockSpec((1, LANES), lambda b, c: (0, c))
    return _strip_call(_ssd_conv_kernel, "ssd_conv", XBC, batch, seq_len, [strip(0), taps, vec],
                       (xbc, w, bias.reshape(1, XBC)))


def shortconv(sc, w, batch, seq_len):
    nblk = D_SC // LANES
    strip = lambda off: pl.BlockSpec((seq_len, LANES), lambda b, c: (b, c + off))
    taps = pl.BlockSpec((SC_CONV, LANES), lambda b, c: (0, c))
    return _strip_call(_shortconv_kernel, "shortconv", D_SC, batch, seq_len,
                       [strip(0), strip(nblk), strip(2 * nblk), taps], (sc, sc, sc, w))


HEAD_PAIRS = D_SSD // LANES
PAIRS_PER_GROUP = HEAD_PAIRS // SSD_GROUPS


def _dot3(x, w):
    return sum(jnp.dot(p, w, preferred_element_type=F32) for p in _split3(x))


def _softplus(x):
    return jnp.maximum(x, 0.0) + jnp.log(1.0 + jnp.exp(-jnp.abs(x)))


def _ssd_scan_kernel(xbc_ref, dt_ref, dtt_ref, bias_ref, a_ref, biast_ref, at_ref, erep_ref, eexp_ref,
                     y_ref, h_scr, cumrep_scr):
    d = pl.program_id(1)
    c = pl.program_id(2)
    q = SSD_CHUNK

    @pl.when(c == 0)
    def _():
        h_scr[...] = jnp.zeros_like(h_scr)

    row = lax.broadcasted_iota(jnp.int32, (q, q), 0)
    col = lax.broadcasted_iota(jnp.int32, (q, q), 1)
    sign = jnp.where(d == 0, 1, -1)
    mask = (row - col) * sign >= 0
    mask01 = jnp.where(mask, 1.0, 0.0).astype(BF16)
    low = lax.broadcasted_iota(jnp.int32, (q, LANES), 1) < HEAD_DIM
    contract_last = (((1,), (1,)), ((), ()))
    contract_first = (((0,), (0,)), ((), ()))

    dt = _softplus(dt_ref[...] + bias_ref[...])
    a = dt * a_ref[...]
    cum = sum(jnp.dot(mask01, p, preferred_element_type=F32) for p in _split3(a))
    eexp = eexp_ref[0]
    cumrep_scr[...] = _dot3(cum, erep_ref[0])
    cum_exp = _dot3(cum, eexp)
    dt_exp = _dot3(dt, eexp)
    total_exp = jnp.sum(_dot3(a, eexp), axis=0, keepdims=True)

    heads = pl.ds(pl.multiple_of(d * SSD_HEADS, SUBLANES), SSD_HEADS)
    at = _softplus(dtt_ref[heads, :] + biast_ref[heads, :]) * at_ref[heads, :]
    cum_t = sum(lax.dot_general(p, mask01, contract_last, preferred_element_type=F32) for p in _split3(at))

    for g in range(SSD_GROUPS):
        b_g = xbc_ref[:, D_SSD + g * SSD_STATE:D_SSD + (g + 1) * SSD_STATE].astype(BF16)
        c_g = xbc_ref[:, D_SSD + (SSD_GROUPS + g) * SSD_STATE:D_SSD + (SSD_GROUPS + g + 1) * SSD_STATE].astype(BF16)
        cb = lax.dot_general(c_g, b_g, contract_last, preferred_element_type=F32)
        for pp in range(PAIRS_PER_GROUP):
            p = g * PAIRS_PER_GROUP + pp
            sl = slice(p * LANES, (p + 1) * LANES)
            xdt = xbc_ref[:, sl] * dt_exp[:, sl]
            xdt_b = xdt.astype(BF16)
            yd = []
            for hh in range(2):
                h = 2 * p + hh
                seg = cumrep_scr[:, h * LANES:(h + 1) * LANES] - cum_t[h:h + 1, :]
                m = (cb * jnp.exp(jnp.where(mask, seg, NEG_INF))).astype(BF16)
                yd.append(jnp.dot(m, xdt_b, preferred_element_type=F32))
            cum_p = cum_exp[:, sl]
            h_in = h_scr[p]
            y_off = jnp.dot(c_g, h_in.astype(BF16), preferred_element_type=F32) * jnp.exp(cum_p)
            y_ref[0, :, sl] = jnp.where(low, yd[0], yd[1]) + y_off
            tot_p = total_exp[:, sl]
            xs_end = (xdt * jnp.exp(tot_p - cum_p)).astype(BF16)
            states = lax.dot_general(b_g, xs_end, contract_first, preferred_element_type=F32)
            h_scr[p] = jnp.exp(tot_p) * h_in + states


def head_expanders():
    erep = np.zeros((2, LANES, SSD_HEADS * LANES), np.float32)
    eexp = np.zeros((2, LANES, D_SSD), np.float32)
    for d in range(2):
        for h in range(SSD_HEADS):
            erep[d, d * SSD_HEADS + h, h * LANES:(h + 1) * LANES] = 1.0
            eexp[d, d * SSD_HEADS + h, h * HEAD_DIM:(h + 1) * HEAD_DIM] = 1.0
    return jnp.asarray(erep, BF16), jnp.asarray(eexp, BF16)


def ssd_scan(xbc_c, dt_raw, dt_bias, a_log, batch, seq_len):
    T = xbc_c.shape[0]
    assert seq_len % SSD_CHUNK == 0
    nc = seq_len // SSD_CHUNK
    lane_pad = LANES - 2 * SSD_HEADS
    dt_l = jnp.pad(dt_raw, ((0, 0), (0, lane_pad)))
    dt_t = dt_raw.T
    bias = dt_bias.reshape(2 * SSD_HEADS).astype(F32)
    a_neg = -jnp.exp(a_log.reshape(2 * SSD_HEADS).astype(F32))
    erep, eexp = head_expanders()

    def chunk(b, d, c):
        return b * nc + jnp.where(d == 0, c, nc - 1 - c)

    vec = pl.BlockSpec((1, LANES), lambda b, d, c: (0, 0))
    colv = pl.BlockSpec((2 * SSD_HEADS, 1), lambda b, d, c: (0, 0))
    return pl.pallas_call(
        _ssd_scan_kernel,
        grid=(batch, 2, nc),
        in_specs=[pl.BlockSpec((SSD_CHUNK, XBC), lambda b, d, c: (chunk(b, d, c), 0)),
                  pl.BlockSpec((SSD_CHUNK, LANES), lambda b, d, c: (chunk(b, d, c), 0)),
                  pl.BlockSpec((2 * SSD_HEADS, SSD_CHUNK), lambda b, d, c: (0, chunk(b, d, c))),
                  vec, vec, colv, colv,
                  pl.BlockSpec((1, LANES, SSD_HEADS * LANES), lambda b, d, c: (d, 0, 0)),
                  pl.BlockSpec((1, LANES, D_SSD), lambda b, d, c: (d, 0, 0))],
        out_specs=pl.BlockSpec((1, SSD_CHUNK, D_SSD), lambda b, d, c: (d, chunk(b, d, c), 0)),
        out_shape=jax.ShapeDtypeStruct((2, T, D_SSD), F32),
        scratch_shapes=[pltpu.VMEM((HEAD_PAIRS, SSD_STATE, LANES), F32),
                        pltpu.VMEM((SSD_CHUNK, SSD_HEADS * LANES), F32)],
        compiler_params=pltpu.CompilerParams(dimension_semantics=("parallel", "arbitrary", "arbitrary"),
                                             vmem_limit_bytes=VMEM_LIMIT_BYTES),
        name="ssd_scan",
    )(xbc_c, dt_l, dt_t, jnp.pad(bias, (0, lane_pad)).reshape(1, LANES),
      jnp.pad(a_neg, (0, lane_pad)).reshape(1, LANES), bias.reshape(2 * SSD_HEADS, 1),
      a_neg.reshape(2 * SSD_HEADS, 1), erep, eexp)


def _ssd_gate_norm_kernel(y_ref, xs_ref, z_ref, dskip_ref, nw_ref, o_ref):
    y = (y_ref[0] + y_ref[1] + xs_ref[...] * dskip_ref[...]) * _silu(z_ref[...])
    gw = D_SSD // SSD_GROUPS
    for g in range(SSD_GROUPS):
        yg = y[:, g * gw:(g + 1) * gw]
        ms = jnp.mean(yg * yg, axis=-1, keepdims=True)
        o_ref[:, g * gw:(g + 1) * gw] = yg * lax.rsqrt(ms + EPS) * nw_ref[:, g * gw:(g + 1) * gw]


def ssd_gate_norm(y2, xbc_c, z, d_skip, norm_w, *, tm=512):
    T = z.shape[0]
    dvec = jnp.repeat(d_skip.astype(F32), HEAD_DIM).reshape(1, D_SSD)
    blk = pl.BlockSpec((tm, D_SSD), lambda i: (i, 0))
    vec = pl.BlockSpec((1, D_SSD), lambda i: (0, 0))
    return pl.pallas_call(
        _ssd_gate_norm_kernel,
        grid=(T // tm,),
        in_specs=[pl.BlockSpec((2, tm, D_SSD), lambda i: (0, i, 0)), blk, blk, vec, vec],
        out_specs=blk,
        out_shape=jax.ShapeDtypeStruct((T, D_SSD), F32),
        compiler_params=pltpu.CompilerParams(dimension_semantics=("parallel",), vmem_limit_bytes=VMEM_LIMIT_BYTES),
        name="ssd_gate_norm",
    )(y2, xbc_c, z, dvec, norm_w.reshape(1, D_SSD))


def ssd_mixer(z, xbc, dt_raw, conv_w, conv_b, dt_bias, a_log, d_skip, norm_w, batch, seq_len):
    xbc_c = ssd_conv(xbc, conv_w, conv_b, batch, seq_len)
    y2 = ssd_scan(xbc_c, dt_raw, dt_bias, a_log, batch, seq_len)
    return ssd_gate_norm(y2, xbc_c, z, d_skip, norm_w)


def trunk(x, norm_mix, w_in, ssd_conv_w, ssd_conv_b, ssd_dt_bias, ssd_a_log, ssd_d, ssd_norm,
          sc_conv_w, sc_norm, att_norm, w_out, norm_ffn, peer_wq, peer_subkeys, peer_u, peer_v, norm_final):
    b, L, D = x.shape
    T = b * L
    x = x.reshape(T, D)
    o_xbc = D_SSD
    o_dt = o_xbc + XBC
    o_sc = o_dt + 2 * SSD_HEADS
    o_att = o_sc + 3 * D_SC
    for l in range(DEPTH):
        w = w_in[l]
        g = norm_mix[l]
        w_dt = jnp.pad(w[:, o_dt:o_sc], ((0, 0), (0, DT_PAD - 2 * SSD_HEADS)))
        z = norm_matmul(x, g, w[:, :o_xbc].astype(BF16))
        xbc = norm_matmul(x, g, w[:, o_xbc:o_dt].astype(BF16))
        dt_raw = norm_matmul(x, g, w_dt.astype(BF16))[:, :2 * SSD_HEADS]
        sc = norm_matmul(x, g, w[:, o_sc:o_att].astype(BF16), tn=768)
        att = norm_matmul(x, g, w[:, o_att:].astype(BF16), tn=768)
        y = jnp.concatenate([
            ssd_mixer(z, xbc, dt_raw, ssd_conv_w[l], ssd_conv_b[l], ssd_dt_bias[l], ssd_a_log[l], ssd_d[l],
                      ssd_norm[l], b, L),
            rmsnorm_rows(shortconv(sc, sc_conv_w[l], b, L), sc_norm[l]),
            attention_mixer(att, att_norm[l], b, L)], axis=-1)
        x = matmul_residual(y, w_out[l].astype(BF16), x)
        q = norm_matmul(x, norm_ffn[l], peer_wq[l].astype(BF16))
        h = rmsnorm_rows(x, norm_ffn[l])
        eidx, gate = peer_topk(q, peer_subkeys[l])
        x = peer_experts(h, x, eidx, gate, pack_table(peer_u[l]), pack_table(peer_v[l]),
                         tile=PEER_TILE, blk=PEER_BLK)
    return rmsnorm_rows(x, norm_final).reshape(b, L, D)


def kernel(x_prompt, x_sample, norm_mix, w_in, ssd_conv_w, ssd_conv_b, ssd_dt_bias, ssd_a_log, ssd_d, ssd_norm,
           sc_conv_w, sc_norm, att_norm, w_out, norm_ffn, peer_wq, peer_subkeys, peer_u, peer_v, norm_final):
    nb = x_prompt.shape[0]
    x = jnp.concatenate([x_prompt, x_sample], axis=0)
    y = trunk(x, norm_mix, w_in, ssd_conv_w, ssd_conv_b, ssd_dt_bias, ssd_a_log, ssd_d, ssd_norm,
              sc_conv_w, sc_norm, att_norm, w_out, norm_ffn, peer_wq, peer_subkeys, peer_u, peer_v, norm_final)
    return (y[:nb], y[nb:])
```

```python
import functools

import jax
import jax.numpy as jnp
import numpy as np
from jax import lax
from jax.experimental import pallas as pl
from jax.experimental.pallas import tpu as pltpu

D_MODEL = 2048
DEPTH = 2
D_MIX = 3 * D_MODEL // 2
HEAD_DIM = 64
D_SSD = D_MIX // 2
SSD_HEADS = D_SSD // HEAD_DIM
SSD_GROUPS = 4
SSD_STATE = 128
SSD_CONV = 5
SSD_CHUNK = 128
XBC = D_SSD + 2 * SSD_GROUPS * SSD_STATE
D_SC = D_MIX // 4
SC_CONV = 3
D_ATT = D_MIX // 4
DILATION_CFG = ((128, 1), (512, 4), (2048, 16))
N_DIL = len(DILATION_CFG)
ROPE_DIM = HEAD_DIM // 4
ROPE_THETA = 500000.0
PEER_HEADS = 8
PEER_NKEYS = 128
PEER_TOPK = 16
PEER_DQ = 256
EPS = 1e-6
NEG = -1e30
DT_PAD = 128

LANES = 128
SUBLANES = 8
D_ROWS = D_MODEL // LANES
P_ROWS = D_ROWS // 2
PEER_TILE = 4096
PEER_BLK = 128
PEER_CHUNK_SLOTS = 256
GROUP_UNROLL = 8
ROUTE_POS_SHIFT = 12
ROUTE_TOK_SHIFT = 20
ROUTE_TILE_SHIFT = 28

VMEM_LIMIT_BYTES = 56 * 1024 * 1024
PEER_VMEM_LIMIT = VMEM_LIMIT_BYTES

F32 = jnp.float32
BF16 = jnp.bfloat16


def _norm_matmul_kernel(x_ref, g_ref, w_ref, o_ref, h_scr):
    @pl.when(pl.program_id(1) == 0)
    def _():
        x = x_ref[...]
        ms = jnp.mean(x * x, axis=-1, keepdims=True)
        h_scr[...] = (x * lax.rsqrt(ms + EPS) * g_ref[...]).astype(BF16)

    o_ref[...] = jnp.dot(h_scr[...], w_ref[...], preferred_element_type=F32)


def norm_matmul(x, g, w, *, tm=512, tn=512):
    m, k = x.shape
    n = w.shape[1]
    tn = min(tn, n)
    assert m % tm == 0 and n % tn == 0
    return pl.pallas_call(
        _norm_matmul_kernel,
        grid=(m // tm, n // tn),
        in_specs=[
            pl.BlockSpec((tm, k), lambda i, j: (i, 0)),
            pl.BlockSpec((1, k), lambda i, j: (0, 0)),
            pl.BlockSpec((k, tn), lambda i, j: (0, j)),
        ],
        out_specs=pl.BlockSpec((tm, tn), lambda i, j: (i, j)),
        out_shape=jax.ShapeDtypeStruct((m, n), F32),
        scratch_shapes=[pltpu.VMEM((tm, k), BF16)],
        compiler_params=pltpu.CompilerParams(
            dimension_semantics=("parallel", "arbitrary"),
            vmem_limit_bytes=VMEM_LIMIT_BYTES),
        name="norm_matmul",
    )(x, g.reshape(1, k), w)


def _matmul_residual_kernel(y_ref, w_ref, r_ref, o_ref):
    o_ref[...] = r_ref[...] + jnp.dot(y_ref[...].astype(BF16), w_ref[...],
                                      preferred_element_type=F32)


def matmul_residual(y, w, res, *, tm=512, tn=512):
    m, k = y.shape
    n = w.shape[1]
    assert m % tm == 0 and n % tn == 0
    return pl.pallas_call(
        _matmul_residual_kernel,
        grid=(m // tm, n // tn),
        in_specs=[
            pl.BlockSpec((tm, k), lambda i, j: (i, 0)),
            pl.BlockSpec((k, tn), lambda i, j: (0, j)),
            pl.BlockSpec((tm, tn), lambda i, j: (i, j)),
        ],
        out_specs=pl.BlockSpec((tm, tn), lambda i, j: (i, j)),
        out_shape=jax.ShapeDtypeStruct((m, n), F32),
        compiler_params=pltpu.CompilerParams(
            dimension_semantics=("parallel", "arbitrary"),
            vmem_limit_bytes=VMEM_LIMIT_BYTES),
        name="matmul_residual",
    )(y, w, res)


def _rmsnorm_kernel(x_ref, g_ref, o_ref):
    x = x_ref[...]
    ms = jnp.mean(x * x, axis=-1, keepdims=True)
    o_ref[...] = x * lax.rsqrt(ms + EPS) * g_ref[...]


def rmsnorm_rows(x, g, *, tm=512):
    m, k = x.shape
    return pl.pallas_call(
        _rmsnorm_kernel,
        grid=(m // tm,),
        in_specs=[pl.BlockSpec((tm, k), lambda i: (i, 0)),
                  pl.BlockSpec((1, k), lambda i: (0, 0))],
        out_specs=pl.BlockSpec((tm, k), lambda i: (i, 0)),
        out_shape=jax.ShapeDtypeStruct((m, k), F32),
        compiler_params=pltpu.CompilerParams(dimension_semantics=("parallel",)),
        name="rmsnorm",
    )(x, g.reshape(1, k))


NCAND = PEER_TOPK * PEER_TOPK
ENC_SHIFT = PEER_NKEYS * PEER_NKEYS
CAND_AB = [(a, b) for a in range(PEER_TOPK) for b in range(PEER_TOPK) if (a + 1) * (b + 1) <= PEER_TOPK]
NCAND_KEPT = -(-len(CAND_AB) // SUBLANES) * SUBLANES
NSEL_EXT = 2 * PEER_TOPK + SUBLANES
NEG_INF = float("-inf")


def candidate_matrices():
    expand = np.zeros((NCAND_KEPT, NSEL_EXT), np.float32)
    scale = np.zeros((NCAND_KEPT, NSEL_EXT), np.float32)
    for j, (a, b) in enumerate(CAND_AB):
        expand[j, a] = expand[j, PEER_TOPK + b] = 1.0
        scale[j, a], scale[j, PEER_TOPK + b] = PEER_NKEYS, 1.0
        scale[j, 2 * PEER_TOPK] = (a * PEER_TOPK + b) * ENC_SHIFT
    expand[len(CAND_AB):, 2 * PEER_TOPK] = -1e30
    scale[len(CAND_AB):, 2 * PEER_TOPK] = (NCAND - 1) * ENC_SHIFT
    return jnp.asarray(expand, BF16), jnp.asarray(scale, BF16)


def _split2(x):
    p0 = x.astype(BF16)
    return p0, (x - p0.astype(F32)).astype(BF16)


def _split3(x):
    p0 = x.astype(BF16)
    r = x - p0.astype(F32)
    p1 = r.astype(BF16)
    p2 = (r - p1.astype(F32)).astype(BF16)
    return p0, p1, p2


def _peer_topk_kernel(q_ref, k_ref, expand_ref, scale_ref, eidx_ref, gate_ref,
                      s_scr, sv_scr, si_scr, c_scr, enc_scr, top_scr, e_scr):
    tm = q_ref.shape[0]
    half = PEER_DQ // 2
    key_f = lax.broadcasted_iota(jnp.int32, (PEER_NKEYS, tm), 0).astype(F32)
    contract_last = (((1,), (1,)), ((), ()))

    for i in range(2):
        q = q_ref[:, i * half:(i + 1) * half].astype(BF16)
        s_scr[i] = lax.dot_general(k_ref[0, i], q, contract_last, preferred_element_type=F32)

    def round1(r, carry):
        for i in range(2):
            s = s_scr[i]
            m = jnp.max(s, axis=0, keepdims=True)
            pos = jnp.min(jnp.where(s == m, key_f, float(PEER_NKEYS)), axis=0, keepdims=True)
            s_scr[i] = jnp.where(key_f == pos, NEG_INF, s)
            sv_scr[pl.ds(i * PEER_TOPK + r, 1), :] = m
            si_scr[pl.ds(i * PEER_TOPK + r, 1), :] = pos
        return carry

    lax.fori_loop(0, PEER_TOPK, round1, 0)

    ones_row = (lax.broadcasted_iota(jnp.int32, (NSEL_EXT - 2 * PEER_TOPK, tm), 0) == 0).astype(F32)
    sv_scr[2 * PEER_TOPK:, :] = ones_row
    si_scr[2 * PEER_TOPK:, :] = ones_row
    c_scr[...] = sum(jnp.dot(expand_ref[...], p, preferred_element_type=F32) for p in _split3(sv_scr[...]))
    enc_scr[...] = jnp.dot(scale_ref[...], si_scr[...].astype(BF16), preferred_element_type=F32)
    big = float(NCAND * ENC_SHIFT)

    def round2(r, carry):
        c = c_scr[...]
        enc = enc_scr[...]
        m = jnp.max(c, axis=0, keepdims=True)
        e = jnp.min(jnp.where(c == m, enc, big), axis=0, keepdims=True)
        c_scr[...] = jnp.where(enc == e, NEG_INF, c)
        top_scr[pl.ds(r, 1), :] = m
        e_scr[pl.ds(r, 1), :] = e
        return carry

    lax.fori_loop(0, PEER_TOPK, round2, 0)
    top = top_scr[...]
    ex = jnp.exp(top - jnp.max(top, axis=0, keepdims=True))
    gate_ref[0] = ex / jnp.sum(ex, axis=0, keepdims=True)
    eidx_ref[0] = e_scr[...].astype(jnp.int32) & (ENC_SHIFT - 1)


def peer_topk(q, subkeys, *, tm=512):
    T = q.shape[0]
    assert T % tm == 0
    vm = pltpu.VMEM
    eidx, gate = pl.pallas_call(
        _peer_topk_kernel,
        grid=(T // tm, PEER_HEADS),
        in_specs=[pl.BlockSpec((tm, PEER_DQ), lambda i, h: (i, h)),
                  pl.BlockSpec((1, 2, PEER_NKEYS, PEER_DQ // 2), lambda i, h: (h, 0, 0, 0)),
                  pl.BlockSpec((NCAND_KEPT, NSEL_EXT), lambda i, h: (0, 0)),
                  pl.BlockSpec((NCAND_KEPT, NSEL_EXT), lambda i, h: (0, 0))],
        out_specs=[pl.BlockSpec((1, PEER_TOPK, tm), lambda i, h: (h, 0, i)),
                   pl.BlockSpec((1, PEER_TOPK, tm), lambda i, h: (h, 0, i))],
        out_shape=[jax.ShapeDtypeStruct((PEER_HEADS, PEER_TOPK, T), jnp.int32),
                   jax.ShapeDtypeStruct((PEER_HEADS, PEER_TOPK, T), F32)],
        scratch_shapes=[vm((2, PEER_NKEYS, tm), F32), vm((NSEL_EXT, tm), F32), vm((NSEL_EXT, tm), F32),
                        vm((NCAND_KEPT, tm), F32), vm((NCAND_KEPT, tm), F32), vm((PEER_TOPK, tm), F32),
                        vm((PEER_TOPK, tm), F32)],
        compiler_params=pltpu.CompilerParams(dimension_semantics=("parallel", "arbitrary")),
        name="peer_topk",
    )(q, subkeys.astype(BF16), *candidate_matrices())
    fix = lambda a: a.transpose(2, 0, 1).reshape(T, PEER_HEADS * PEER_TOPK)
    return fix(eidx), fix(gate)


def _fold(xa, xb, mask, shift):
    s = jnp.where(mask, xa, xb)
    t = jnp.where(mask, xb, xa)
    return s + pltpu.roll(t, shift, axis=0)


def _unpack(w):
    lo = lax.bitcast_convert_type(w << 16, F32)
    hi = lax.bitcast_convert_type(w & jnp.int32(-65536), F32)
    return lo, hi


def _peer_dot_kernel(meta_ref, *refs, chunk):
    row_refs = refs[:SUBLANES]
    gtok_ref, x_ref, gate_ref, u_ref, a_ref, c_scr = refs[SUBLANES:]
    k = pl.program_id(1)
    gpc = chunk // SUBLANES
    c0 = meta_ref[0, 0, k]
    c1 = meta_ref[0, 0, k + 1]

    @pl.when(k == 0)
    def _():
        a_ref[...] = jnp.zeros_like(a_ref)

    c_scr[...] = jnp.zeros_like(c_scr)
    row_i = lax.broadcasted_iota(jnp.int32, (SUBLANES, LANES), 0)
    m1 = (row_i & 1) == 0
    m2 = (row_i & 2) == 0
    m3 = (row_i & 4) == 0
    ones = jnp.ones((SUBLANES, LANES), BF16)
    contract_last = (((1,), (1,)), ((), ()))

    def finish(c, buf):
        h8 = sum(lax.dot_general(ones, p, contract_last, preferred_element_type=F32) for p in _split2(c_scr[buf]))
        h = h8[0:1, :]
        off = pl.multiple_of(c * chunk, chunk)
        act = 0.5 * h * (1.0 + lax.erf(h * 0.7071067811865476))
        a_ref[0, :, pl.ds(off, chunk)] = act * gate_ref[0, :, pl.ds(off, chunk)]

    def chunk_body(c, carry):
        buf = (c - c0) & 1
        finish(jnp.maximum(c - 1, c0), 1 - buf)
        for gl in range(gpc):
            g = c * gpc + gl
            tok = gtok_ref[0, 0, g]
            xlo = x_ref[tok, 0:P_ROWS, :]
            xhi = x_ref[tok, P_ROWS:D_ROWS, :]
            ps = []
            for i in range(SUBLANES):
                r8 = pl.multiple_of(row_refs[i][0, 0, g], SUBLANES)
                lo, hi = _unpack(u_ref[pl.ds(r8, SUBLANES), :])
                ps.append(lo * xlo + hi * xhi)
            q = [_fold(ps[2 * i], ps[2 * i + 1], m1, 1) for i in range(4)]
            r = [_fold(q[0], q[1], m2, 2), _fold(q[2], q[3], m2, 2)]
            c_scr[buf, gl * SUBLANES:(gl + 1) * SUBLANES, :] = _fold(r[0], r[1], m3, 4)
        return carry

    lax.fori_loop(c0, c1, chunk_body, 0)

    @pl.when(c1 > c0)
    def _():
        finish(c1 - 1, (c1 - 1 - c0) & 1)


def _peer_acc_kernel(meta_ref, *refs, chunk):
    row_refs = refs[:SUBLANES]
    a_refs = refs[SUBLANES:2 * SUBLANES]
    gtok_ref, res_ref, v_ref, o_ref = refs[2 * SUBLANES:]
    k = pl.program_id(1)
    gpc = chunk // SUBLANES

    @pl.when(k == 0)
    def _():
        o_ref[...] = res_ref[...]

    def groups(gp, carry):
        for sub in range(GROUP_UNROLL):
            g = gp * GROUP_UNROLL + sub
            tok = gtok_ref[0, 0, g]
            alo = ahi = None
            for i in range(SUBLANES):
                r8 = pl.multiple_of(row_refs[i][0, 0, g], SUBLANES)
                lo, hi = _unpack(v_ref[pl.ds(r8, SUBLANES), :])
                a = a_refs[i][0, 0, g]
                alo = a * lo if alo is None else alo + a * lo
                ahi = a * hi if ahi is None else ahi + a * hi
            o_ref[tok, 0:P_ROWS, :] += alo
            o_ref[tok, P_ROWS:D_ROWS, :] += ahi
        return carry

    per_chunk = gpc // GROUP_UNROLL
    lax.fori_loop(meta_ref[0, 0, k] * per_chunk, meta_ref[0, 0, k + 1] * per_chunk, groups, 0)


def pack_table(w):
    e = w.shape[0]
    bits = lax.bitcast_convert_type(w.astype(BF16), jnp.uint16).astype(jnp.uint32)
    half = D_MODEL // 2
    packed = (bits[:, half:] << 16) | bits[:, :half]
    return lax.bitcast_convert_type(packed, jnp.int32).reshape(e * P_ROWS, LANES)


def by_lane(a):
    nb, cap = a.shape[0], a.shape[-1]
    a = a.reshape(nb, cap // SUBLANES, SUBLANES)
    return [a[:, :, i].reshape(nb, 1, cap // SUBLANES) for i in range(SUBLANES)]


def peer_route(eidx, gate, *, n_exp, tile, blk, chunk):
    T, S = eidx.shape
    nb = T // blk
    nt = n_exp // tile
    shift = tile.bit_length() - 1
    assert 1 << shift == tile and T % blk == 0 and S + SUBLANES <= 256 and chunk <= 256
    assert blk & (blk - 1) == 0 and blk < 256 and shift <= ROUTE_POS_SHIFT and nt <= 4
    i32 = jnp.int32
    tiles = jnp.arange(nt, dtype=i32)
    tile_id = eidx >> shift
    cnt = jnp.sum(tile_id[:, :, None] == tiles[None, None, :], axis=1, dtype=i32)
    tpad = (-cnt) % SUBLANES
    t_in = (jnp.arange(T, dtype=i32) % blk)[:, None]
    dead = i32(nt << ROUTE_TILE_SHIFT)
    tile_f, tok_f, pos_f = ROUTE_TILE_SHIFT, ROUTE_TOK_SHIFT, ROUTE_POS_SHIFT
    key_real = ((tile_id << tile_f) | (t_in << tok_f) | (jnp.arange(S, dtype=i32)[None, :] << pos_f)
                | (eidx & (tile - 1)))
    i8 = jnp.arange(SUBLANES, dtype=i32)
    key_tpad = jnp.where(i8[None, None, :] < tpad[:, :, None],
                         (tiles[None, :, None] << tile_f) | (t_in[:, :, None] << tok_f)
                         | ((S + i8)[None, None, :] << pos_f), dead)
    n_run = jnp.sum((cnt + tpad).reshape(nb, blk, nt), axis=1)
    bpad = (-n_run) % chunk
    ib = jnp.arange(chunk - SUBLANES, dtype=i32)
    key_bpad = jnp.where(ib[None, None, :] < bpad[:, :, None],
                         (tiles[None, :, None] << tile_f) | i32(blk << tok_f) | (ib[None, None, :] << pos_f), dead)
    n_real, n_tp, n_bp = blk * S, blk * nt * SUBLANES, nt * (chunk - SUBLANES)
    cap = -(-(n_real + n_tp + nt * chunk) // 1024) * 1024
    fill = cap - (n_real + n_tp + n_bp)
    keys = jnp.concatenate([key_real.reshape(nb, n_real), key_tpad.reshape(nb, n_tp), key_bpad.reshape(nb, n_bp),
                            jnp.full((nb, fill), dead, i32)], axis=1)
    gates = jnp.concatenate([gate.reshape(nb, n_real), jnp.zeros((nb, cap - n_real), gate.dtype)], axis=1)
    keys, gates = lax.sort((keys, gates), dimension=1, num_keys=1)
    rows = (keys & ((1 << pos_f) - 1)) * SUBLANES
    gtok = (keys[:, ::SUBLANES] >> tok_f) & (blk - 1)
    n_chunks = (n_run + bpad) // chunk
    cstart = jnp.concatenate([jnp.zeros((nb, 1), i32), jnp.cumsum(n_chunks, axis=1)], axis=1)
    meta = jnp.pad(cstart, ((0, 0), (0, LANES - nt - 1)))
    r3 = lambda a: a.reshape(nb, 1, a.shape[1])
    return r3(meta), by_lane(rows), r3(gtok), r3(gates), cap


def peer_experts(h, res, eidx, gate, u_packed, v_packed, *, tile, blk):
    T = h.shape[0]
    chunk = PEER_CHUNK_SLOTS
    n_exp = u_packed.shape[0] // P_ROWS
    nt = n_exp // tile
    nb = T // blk
    meta, rows, gtok, gates, cap = peer_route(eidx, gate, n_exp=n_exp, tile=tile, blk=blk, chunk=chunk)
    tok_spec = pl.BlockSpec((blk, D_ROWS, LANES), lambda j, k: (j, 0, 0))
    tab_spec = pl.BlockSpec((tile * P_ROWS, LANES), lambda j, k: (k, 0))
    smem = lambda n: pl.BlockSpec((1, 1, n), lambda j, k: (j, 0, 0), memory_space=pltpu.SMEM,
                                  pipeline_mode=pl.Buffered(1))
    slot_spec = pl.BlockSpec((1, 1, cap), lambda j, k: (j, 0, 0))
    params = pltpu.CompilerParams(dimension_semantics=("parallel", "arbitrary"), vmem_limit_bytes=PEER_VMEM_LIMIT)
    a = pl.pallas_call(
        functools.partial(_peer_dot_kernel, chunk=chunk),
        grid=(nb, nt),
        in_specs=[smem(LANES)] + [smem(cap // SUBLANES)] * (SUBLANES + 1) + [tok_spec, slot_spec, tab_spec],
        out_specs=slot_spec,
        out_shape=jax.ShapeDtypeStruct((nb, 1, cap), F32),
        scratch_shapes=[pltpu.VMEM((2, chunk, LANES), F32)],
        compiler_params=params, name="peer_dot",
    )(meta, *rows, gtok, h.reshape(T, D_ROWS, LANES), gates, u_packed)
    out = pl.pallas_call(
        functools.partial(_peer_acc_kernel, chunk=chunk),
        grid=(nb, nt),
        in_specs=[smem(LANES)] + [smem(cap // SUBLANES)] * (2 * SUBLANES + 1) + [tok_spec, tab_spec],
        out_specs=tok_spec,
        out_shape=jax.ShapeDtypeStruct((T, D_ROWS, LANES), F32),
        compiler_params=params, name="peer_acc",
    )(meta, *rows, *by_lane(a.reshape(nb, cap)), gtok, res.reshape(T, D_ROWS, LANES), v_packed)
    return out.reshape(T, D_MODEL)


ATT_W = 3 * N_DIL * D_ATT
ROPE_HALF = ROPE_DIM // 2
ATT_TQ = 128
N_QK_BLOCKS = 2 * N_DIL


def rope_lane_tables(length):
    inv = ROPE_THETA ** (-jnp.arange(0, ROPE_DIM, 2, dtype=F32) / ROPE_DIM)
    ang = jnp.arange(length, dtype=F32)[:, None] * inv[None, :]
    cos, sin = jnp.cos(ang), jnp.sin(ang)
    lane = np.arange(LANES) % HEAD_DIM
    j = lane % ROPE_HALF
    first, second = lane < ROPE_HALF, (lane >= ROPE_HALF) & (lane < ROPE_DIM)
    c = jnp.where((first | second)[None, :], cos[:, j], 1.0)
    s1 = jnp.where(second[None, :], sin[:, j], 0.0)
    s2 = jnp.where(first[None, :], -sin[:, j], 0.0)
    return c, s1, s2


def _rope_cast_kernel(x_ref, c_ref, s1_ref, s2_ref, o_ref, *, q_scale):
    col = pl.program_id(1)

    @pl.when(col < N_QK_BLOCKS)
    def _():
        scale = jnp.where(col < N_DIL, q_scale, 1.0)
        c, s1, s2 = c_ref[...], s1_ref[...], s2_ref[...]
        for t in range(D_ATT // LANES):
            x = x_ref[:, t * LANES:(t + 1) * LANES]
            y = x * c + pltpu.roll(x, ROPE_HALF, axis=1) * s1 + pltpu.roll(x, LANES - ROPE_HALF, axis=1) * s2
            o_ref[:, t * LANES:(t + 1) * LANES] = (y * scale).astype(BF16)

    @pl.when(col >= N_QK_BLOCKS)
    def _():
        o_ref[...] = x_ref[...].astype(BF16)


def rope_cast(att, seq_len, *, tm=512):
    T = att.shape[0]
    assert seq_len % tm == 0
    c, s1, s2 = rope_lane_tables(seq_len)
    nblk = seq_len // tm
    tab = pl.BlockSpec((tm, LANES), lambda i, j: (i % nblk, 0))
    blk = pl.BlockSpec((tm, D_ATT), lambda i, j: (i, j))
    return pl.pallas_call(
        functools.partial(_rope_cast_kernel, q_scale=HEAD_DIM ** -0.5),
        grid=(T // tm, ATT_W // D_ATT),
        in_specs=[blk, tab, tab, tab],
        out_specs=blk,
        out_shape=jax.ShapeDtypeStruct((T, ATT_W), BF16),
        compiler_params=pltpu.CompilerParams(dimension_semantics=("parallel", "parallel")),
        name="rope_cast",
    )(att, c, s1, s2)


def _band_attn_kernel(q_ref, kp_ref, kc_ref, kn_ref, vp_ref, vc_ref, vn_ref, o_ref, lse_ref, *, hw, sub_len):
    i = pl.program_id(2)
    tq = q_ref.shape[0]
    q_pos = i * tq + lax.broadcasted_iota(jnp.int32, (tq, 3 * tq), 0)
    k_pos = (i - 1) * tq + lax.broadcasted_iota(jnp.int32, (tq, 3 * tq), 1)
    valid = (jnp.abs(k_pos - q_pos) <= hw) & (k_pos >= 0) & (k_pos < sub_len)
    low = lax.broadcasted_iota(jnp.int32, (tq, LANES), 1) < HEAD_DIM
    contract_last = (((1,), (1,)), ((), ()))
    k_refs = (kp_ref, kc_ref, kn_ref)
    v_refs = (vp_ref, vc_ref, vn_ref)
    for pair in range(D_ATT // LANES):
        sl = slice(pair * LANES, (pair + 1) * LANES)
        q2 = q_ref[:, sl]
        ks = [r[:, sl] for r in k_refs]
        vs = [r[:, sl] for r in v_refs]
        out = jnp.zeros((tq, LANES), F32)
        lse2 = jnp.zeros((tq, LANES), F32)
        for head_lanes in (low, ~low):
            qh = jnp.where(head_lanes, q2, jnp.zeros_like(q2))
            s = jnp.concatenate([lax.dot_general(qh, kj, contract_last, preferred_element_type=F32) for kj in ks],
                                axis=1)
            s = jnp.where(valid, s, NEG)
            m = jnp.max(s, axis=-1, keepdims=True)
            p = jnp.exp(s - m)
            l = jnp.sum(p, axis=-1, keepdims=True)
            pb = p.astype(BF16)
            o = sum(jnp.dot(pb[:, j * tq:(j + 1) * tq], vs[j], preferred_element_type=F32) for j in range(3))
            out = jnp.where(head_lanes, o / l, out)
            lse2 = jnp.where(head_lanes, m + jnp.log(l), lse2)
        o_ref[:, sl] = out
        lse_ref[:, sl] = lse2


def band_attention(qkv, g, batch, seq_len, *, tq=ATT_TQ):
    win, dil = DILATION_CFG[g]
    hw = win // (2 * dil)
    sub_len = seq_len // dil
    assert sub_len % tq == 0 and hw <= tq
    nq = sub_len // tq
    ncol = ATT_W // D_ATT
    x = qkv.reshape(batch, sub_len, dil * ATT_W)

    def spec(kind, shift):
        def index(b, r, i):
            return (b, jnp.clip(i + shift, 0, nq - 1), r * ncol + kind * N_DIL + g)
        return pl.BlockSpec((None, tq, D_ATT), index)

    out_spec = pl.BlockSpec((None, tq, D_ATT), lambda b, r, i: (b, i, r))
    o, lse = pl.pallas_call(
        functools.partial(_band_attn_kernel, hw=hw, sub_len=sub_len),
        grid=(batch, dil, nq),
        in_specs=[spec(0, 0), spec(1, -1), spec(1, 0), spec(1, 1), spec(2, -1), spec(2, 0), spec(2, 1)],
        out_specs=[out_spec, out_spec],
        out_shape=[jax.ShapeDtypeStruct((batch, sub_len, dil * D_ATT), F32)] * 2,
        compiler_params=pltpu.CompilerParams(dimension_semantics=("parallel", "parallel", "parallel"),
                                             vmem_limit_bytes=VMEM_LIMIT_BYTES),
        name=f"band_attn_d{dil}",
    )(x, x, x, x, x, x, x)
    return o.reshape(batch * seq_len, D_ATT), lse.reshape(batch * seq_len, D_ATT)


def _attn_merge_kernel(o0, o1, o2, l0, l1, l2, g_ref, y_ref):
    ls = [l0[...], l1[...], l2[...]]
    m = jnp.maximum(jnp.maximum(ls[0], ls[1]), ls[2])
    es = [jnp.exp(l - m) for l in ls]
    den = es[0] + es[1] + es[2]
    o = (es[0] * o0[...] + es[1] * o1[...] + es[2] * o2[...]) / den
    ms = jnp.mean(o * o, axis=-1, keepdims=True)
    y_ref[...] = o * lax.rsqrt(ms + EPS) * g_ref[...]


def attn_merge(outs, lses, norm_w, *, tm=512):
    T = outs[0].shape[0]
    blk = pl.BlockSpec((tm, D_ATT), lambda i: (i, 0))
    return pl.pallas_call(
        _attn_merge_kernel,
        grid=(T // tm,),
        in_specs=[blk] * 6 + [pl.BlockSpec((1, D_ATT), lambda i: (0, 0))],
        out_specs=blk,
        out_shape=jax.ShapeDtypeStruct((T, D_ATT), F32),
        compiler_params=pltpu.CompilerParams(dimension_semantics=("parallel",)),
        name="attn_merge",
    )(*outs, *lses, norm_w.reshape(1, D_ATT))


def attention_mixer(att, norm_w, batch, seq_len):
    qkv = rope_cast(att, seq_len)
    res = [band_attention(qkv, g, batch, seq_len) for g in range(N_DIL)]
    return attn_merge([r[0] for r in res], [r[1] for r in res], norm_w)


CONV_HALO = SUBLANES


def _conv_taps(x, w_ref, pad_scr, width):
    n = x.shape[0]
    half = width // 2
    zeros = jnp.zeros((CONV_HALO, LANES), F32)
    pad_scr[0:CONV_HALO, :] = zeros
    pad_scr[CONV_HALO + n:CONV_HALO + n + CONV_HALO, :] = zeros
    pad_scr[CONV_HALO:CONV_HALO + n, :] = x
    acc = jnp.zeros((n, LANES), F32)
    for k in range(width):
        acc = acc + pad_scr[CONV_HALO - half + k:CONV_HALO - half + k + n, :] * w_ref[k:k + 1, :]
    return acc


def _silu(x):
    return x * (1.0 / (1.0 + jnp.exp(-x)))


def _ssd_conv_kernel(x_ref, w_ref, b_ref, o_ref, pad_scr):
    o_ref[...] = _silu(_conv_taps(x_ref[...], w_ref, pad_scr, SSD_CONV) + b_ref[...])


def _shortconv_kernel(bg_ref, cg_ref, hx_ref, w_ref, o_ref, pad_scr):
    o_ref[...] = bg_ref[...] * _conv_taps(cg_ref[...] * hx_ref[...], w_ref, pad_scr, SC_CONV)


def _strip_call(kernel, name, n_out, batch, seq_len, in_specs, args):
    return pl.pallas_call(
        kernel,
        grid=(batch, n_out // LANES),
        in_specs=in_specs,
        out_specs=pl.BlockSpec((seq_len, LANES), lambda b, c: (b, c)),
        out_shape=jax.ShapeDtypeStruct((batch * seq_len, n_out), F32),
        scratch_shapes=[pltpu.VMEM((seq_len + 2 * CONV_HALO, LANES), F32)],
        compiler_params=pltpu.CompilerParams(dimension_semantics=("parallel", "parallel"),
                                             vmem_limit_bytes=VMEM_LIMIT_BYTES),
        name=name,
    )(*args)


def ssd_conv(xbc, w, bias, batch, seq_len):
    strip = lambda off: pl.BlockSpec((seq_len, LANES), lambda b, c: (b, c + off))
    taps = pl.BlockSpec((SSD_CONV, LANES), lambda b, c: (0, c))
    vec = pl.BlockSpec((1, LANES), lambda b, c: (0, c))
    return _strip_call(_ssd_conv_kernel, "ssd_conv", XBC, batch, seq_len, [strip(0), taps, vec],
                       (xbc, w, bias.reshape(1, XBC)))


def shortconv(sc, w, batch, seq_len):
    nblk = D_SC // LANES
    strip = lambda off: pl.BlockSpec((seq_len, LANES), lambda b, c: (b, c + off))
    taps = pl.BlockSpec((SC_CONV, LANES), lambda b, c: (0, c))
    return _strip_call(_shortconv_kernel, "shortconv", D_SC, batch, seq_len,
                       [strip(0), strip(nblk), strip(2 * nblk), taps], (sc, sc, sc, w))


HEAD_PAIRS = D_SSD // LANES
PAIRS_PER_GROUP = HEAD_PAIRS // SSD_GROUPS


def _dot3(x, w):
    return sum(jnp.dot(p, w, preferred_element_type=F32) for p in _split3(x))


def _softplus(x):
    return jnp.maximum(x, 0.0) + jnp.log(1.0 + jnp.exp(-jnp.abs(x)))


def _ssd_scan_kernel(xbc_ref, dt_ref, dtt_ref, bias_ref, a_ref, biast_ref, at_ref, erep_ref, eexp_ref,
                     y_ref, h_scr, cumrep_scr):
    d = pl.program_id(1)
    c = pl.program_id(2)
    q = SSD_CHUNK

    @pl.when(c == 0)
    def _():
        h_scr[...] = jnp.zeros_like(h_scr)

    row = lax.broadcasted_iota(jnp.int32, (q, q), 0)
    col = lax.broadcasted_iota(jnp.int32, (q, q), 1)
    sign = jnp.where(d == 0, 1, -1)
    mask = (row - col) * sign >= 0
    mask01 = jnp.where(mask, 1.0, 0.0).astype(BF16)
    low = lax.broadcasted_iota(jnp.int32, (q, LANES), 1) < HEAD_DIM
    contract_last = (((1,), (1,)), ((), ()))
    contract_first = (((0,), (0,)), ((), ()))

    dt = _softplus(dt_ref[...] + bias_ref[...])
    a = dt * a_ref[...]
    cum = sum(jnp.dot(mask01, p, preferred_element_type=F32) for p in _split3(a))
    eexp = eexp_ref[0]
    cumrep_scr[...] = _dot3(cum, erep_ref[0])
    cum_exp = _dot3(cum, eexp)
    dt_exp = _dot3(dt, eexp)
    total_exp = jnp.sum(_dot3(a, eexp), axis=0, keepdims=True)

    heads = pl.ds(pl.multiple_of(d * SSD_HEADS, SUBLANES), SSD_HEADS)
    at = _softplus(dtt_ref[heads, :] + biast_ref[heads, :]) * at_ref[heads, :]
    cum_t = sum(lax.dot_general(p, mask01, contract_last, preferred_element_type=F32) for p in _split3(at))

    for g in range(SSD_GROUPS):
        b_g = xbc_ref[:, D_SSD + g * SSD_STATE:D_SSD + (g + 1) * SSD_STATE].astype(BF16)
        c_g = xbc_ref[:, D_SSD + (SSD_GROUPS + g) * SSD_STATE:D_SSD + (SSD_GROUPS + g + 1) * SSD_STATE].astype(BF16)
        cb = lax.dot_general(c_g, b_g, contract_last, preferred_element_type=F32)
        for pp in range(PAIRS_PER_GROUP):
            p = g * PAIRS_PER_GROUP + pp
            sl = slice(p * LANES, (p + 1) * LANES)
            xdt = xbc_ref[:, sl] * dt_exp[:, sl]
            xdt_b = xdt.astype(BF16)
            yd = []
            for hh in range(2):
                h = 2 * p + hh
                seg = cumrep_scr[:, h * LANES:(h + 1) * LANES] - cum_t[h:h + 1, :]
                m = (cb * jnp.exp(jnp.where(mask, seg, NEG_INF))).astype(BF16)
                yd.append(jnp.dot(m, xdt_b, preferred_element_type=F32))
            cum_p = cum_exp[:, sl]
            h_in = h_scr[p]
            y_off = jnp.dot(c_g, h_in.astype(BF16), preferred_element_type=F32) * jnp.exp(cum_p)
            y_ref[0, :, sl] = jnp.where(low, yd[0], yd[1]) + y_off
            tot_p = total_exp[:, sl]
            xs_end = (xdt * jnp.exp(tot_p - cum_p)).astype(BF16)
            states = lax.dot_general(b_g, xs_end, contract_first, preferred_element_type=F32)
            h_scr[p] = jnp.exp(tot_p) * h_in + states


def head_expanders():
    erep = np.zeros((2, LANES, SSD_HEADS * LANES), np.float32)
    eexp = np.zeros((2, LANES, D_SSD), np.float32)
    for d in range(2):
        for h in range(SSD_HEADS):
            erep[d, d * SSD_HEADS + h, h * LANES:(h + 1) * LANES] = 1.0
            eexp[d, d * SSD_HEADS + h, h * HEAD_DIM:(h + 1) * HEAD_DIM] = 1.0
    return jnp.asarray(erep, BF16), jnp.asarray(eexp, BF16)


def ssd_scan(xbc_c, dt_raw, dt_bias, a_log, batch, seq_len):
    T = xbc_c.shape[0]
    assert seq_len % SSD_CHUNK == 0
    nc = seq_len // SSD_CHUNK
    lane_pad = LANES - 2 * SSD_HEADS
    dt_l = jnp.pad(dt_raw, ((0, 0), (0, lane_pad)))
    dt_t = dt_raw.T
    bias = dt_bias.reshape(2 * SSD_HEADS).astype(F32)
    a_neg = -jnp.exp(a_log.reshape(2 * SSD_HEADS).astype(F32))
    erep, eexp = head_expanders()

    def chunk(b, d, c):
        return b * nc + jnp.where(d == 0, c, nc - 1 - c)

    vec = pl.BlockSpec((1, LANES), lambda b, d, c: (0, 0))
    colv = pl.BlockSpec((2 * SSD_HEADS, 1), lambda b, d, c: (0, 0))
    return pl.pallas_call(
        _ssd_scan_kernel,
        grid=(batch, 2, nc),
        in_specs=[pl.BlockSpec((SSD_CHUNK, XBC), lambda b, d, c: (chunk(b, d, c), 0)),
                  pl.BlockSpec((SSD_CHUNK, LANES), lambda b, d, c: (chunk(b, d, c), 0)),
                  pl.BlockSpec((2 * SSD_HEADS, SSD_CHUNK), lambda b, d, c: (0, chunk(b, d, c))),
                  vec, vec, colv, colv,
                  pl.BlockSpec((1, LANES, SSD_HEADS * LANES), lambda b, d, c: (d, 0, 0)),
                  pl.BlockSpec((1, LANES, D_SSD), lambda b, d, c: (d, 0, 0))],
        out_specs=pl.BlockSpec((1, SSD_CHUNK, D_SSD), lambda b, d, c: (d, chunk(b, d, c), 0)),
        out_shape=jax.ShapeDtypeStruct((2, T, D_SSD), F32),
        scratch_shapes=[pltpu.VMEM((HEAD_PAIRS, SSD_STATE, LANES), F32),
                        pltpu.VMEM((SSD_CHUNK, SSD_HEADS * LANES), F32)],
        compiler_params=pltpu.CompilerParams(dimension_semantics=("parallel", "arbitrary", "arbitrary"),
                                             vmem_limit_bytes=VMEM_LIMIT_BYTES),
        name="ssd_scan",
    )(xbc_c, dt_l, dt_t, jnp.pad(bias, (0, lane_pad)).reshape(1, LANES),
      jnp.pad(a_neg, (0, lane_pad)).reshape(1, LANES), bias.reshape(2 * SSD_HEADS, 1),
      a_neg.reshape(2 * SSD_HEADS, 1), erep, eexp)


def _ssd_gate_norm_kernel(y_ref, xs_ref, z_ref, dskip_ref, nw_ref, o_ref):
    y = (y_ref[0] + y_ref[1] + xs_ref[...] * dskip_ref[...]) * _silu(z_ref[...])
    gw = D_SSD // SSD_GROUPS
    for g in range(SSD_GROUPS):
        yg = y[:, g * gw:(g + 1) * gw]
        ms = jnp.mean(yg * yg, axis=-1, keepdims=True)
        o_ref[:, g * gw:(g + 1) * gw] = yg * lax.rsqrt(ms + EPS) * nw_ref[:, g * gw:(g + 1) * gw]


def ssd_gate_norm(y2, xbc_c, z, d_skip, norm_w, *, tm=512):
    T = z.shape[0]
    dvec = jnp.repeat(d_skip.astype(F32), HEAD_DIM).reshape(1, D_SSD)
    blk = pl.BlockSpec((tm, D_SSD), lambda i: (i, 0))
    vec = pl.BlockSpec((1, D_SSD), lambda i: (0, 0))
    return pl.pallas_call(
        _ssd_gate_norm_kernel,
        grid=(T // tm,),
        in_specs=[pl.BlockSpec((2, tm, D_SSD), lambda i: (0, i, 0)), blk, blk, vec, vec],
        out_specs=blk,
        out_shape=jax.ShapeDtypeStruct((T, D_SSD), F32),
        compiler_params=pltpu.CompilerParams(dimension_semantics=("parallel",), vmem_limit_bytes=VMEM_LIMIT_BYTES),
        name="ssd_gate_norm",
    )(y2, xbc_c, z, dvec, norm_w.reshape(1, D_SSD))


def ssd_mixer(z, xbc, dt_raw, conv_w, conv_b, dt_bias, a_log, d_skip, norm_w, batch, seq_len):
    xbc_c = ssd_conv(xbc, conv_w, conv_b, batch, seq_len)
    y2 = ssd_scan(xbc_c, dt_raw, dt_bias, a_log, batch, seq_len)
    return ssd_gate_norm(y2, xbc_c, z, d_skip, norm_w)


def trunk(x, norm_mix, w_in, ssd_conv_w, ssd_conv_b, ssd_dt_bias, ssd_a_log, ssd_d, ssd_norm,
          sc_conv_w, sc_norm, att_norm, w_out, norm_ffn, peer_wq, peer_subkeys, peer_u, peer_v, norm_final):
    b, L, D = x.shape
    T = b * L
    x = x.reshape(T, D)
    o_xbc = D_SSD
    o_dt = o_xbc + XBC
    o_sc = o_dt + 2 * SSD_HEADS
    o_att = o_sc + 3 * D_SC
    for l in range(DEPTH):
        w = w_in[l]
        g = norm_mix[l]
        w_dt = jnp.pad(w[:, o_dt:o_sc], ((0, 0), (0, DT_PAD - 2 * SSD_HEADS)))
        z = norm_matmul(x, g, w[:, :o_xbc].astype(BF16))
        xbc = norm_matmul(x, g, w[:, o_xbc:o_dt].astype(BF16))
        dt_raw = norm_matmul(x, g, w_dt.astype(BF16))[:, :2 * SSD_HEADS]
        sc = norm_matmul(x, g, w[:, o_sc:o_att].astype(BF16), tn=768)
        att = norm_matmul(x, g, w[:, o_att:].astype(BF16), tn=768)
        y = jnp.concatenate([
            ssd_mixer(z, xbc, dt_raw, ssd_conv_w[l], ssd_conv_b[l], ssd_dt_bias[l], ssd_a_log[l], ssd_d[l],
                      ssd_norm[l], b, L),
            rmsnorm_rows(shortconv(sc, sc_conv_w[l], b, L), sc_norm[l]),
            attention_mixer(att, att_norm[l], b, L)], axis=-1)
        x = matmul_residual(y, w_out[l].astype(BF16), x)
        q = norm_matmul(x, norm_ffn[l], peer_wq[l].astype(BF16))
        h = rmsnorm_rows(x, norm_ffn[l])
        eidx, gate = peer_topk(q, peer_subkeys[l])
        x = peer_experts(h, x, eidx, gate, pack_table(peer_u[l]), pack_table(peer_v[l]),
                         tile=PEER_TILE, blk=PEER_BLK)
    return rmsnorm_rows(x, norm_final).reshape(b, L, D)


def kernel(x_prompt, x_sample, norm_mix, w_in, ssd_conv_w, ssd_conv_b, ssd_dt_bias, ssd_a_log, ssd_d, ssd_norm,
           sc_conv_w, sc_norm, att_norm, w_out, norm_ffn, peer_wq, peer_subkeys, peer_u, peer_v, norm_final):
    nb = x_prompt.shape[0]
    x = jnp.concatenate([x_prompt, x_sample], axis=0)
    y = trunk(x, norm_mix, w_in, ssd_conv_w, ssd_conv_b, ssd_dt_bias, ssd_a_log, ssd_d, ssd_norm,
              sc_conv_w, sc_norm, att_norm, w_out, norm_ffn, peer_wq, peer_subkeys, peer_u, peer_v, norm_final)
    return (y[:nb], y[nb:])
```

```python
import functools

import jax
import jax.numpy as jnp
import numpy as np
from jax import lax
from jax.experimental import pallas as pl
from jax.experimental.pallas import tpu as pltpu

D_MODEL = 2048
DEPTH = 2
D_MIX = 3 * D_MODEL // 2
HEAD_DIM = 64
D_SSD = D_MIX // 2
SSD_HEADS = D_SSD // HEAD_DIM
SSD_GROUPS = 4
SSD_STATE = 128
SSD_CONV = 5
SSD_CHUNK = 128
XBC = D_SSD + 2 * SSD_GROUPS * SSD_STATE
D_SC = D_MIX // 4
SC_CONV = 3
D_ATT = D_MIX // 4
DILATION_CFG = ((128, 1), (512, 4), (2048, 16))
N_DIL = len(DILATION_CFG)
ROPE_DIM = HEAD_DIM // 4
ROPE_THETA = 500000.0
PEER_HEADS = 8
PEER_NKEYS = 128
PEER_TOPK = 16
PEER_DQ = 256
EPS = 1e-6
NEG = -1e30
DT_PAD = 128

LANES = 128
SUBLANES = 8
D_ROWS = D_MODEL // LANES
P_ROWS = D_ROWS // 2
PEER_TILE = 4096
PEER_BLK = 128
PEER_CHUNK_SLOTS = 256
GROUP_UNROLL = 8
ROUTE_POS_SHIFT = 12
ROUTE_TOK_SHIFT = 20
ROUTE_TILE_SHIFT = 28

VMEM_LIMIT_BYTES = 56 * 1024 * 1024
PEER_VMEM_LIMIT = VMEM_LIMIT_BYTES

F32 = jnp.float32
BF16 = jnp.bfloat16


def _norm_matmul_kernel(x_ref, g_ref, w_ref, o_ref, h_scr):
    @pl.when(pl.program_id(1) == 0)
    def _():
        x = x_ref[...]
        ms = jnp.mean(x * x, axis=-1, keepdims=True)
        h_scr[...] = (x * lax.rsqrt(ms + EPS) * g_ref[...]).astype(BF16)

    o_ref[...] = jnp.dot(h_scr[...], w_ref[...], preferred_element_type=F32)


def norm_matmul(x, g, w, *, tm=1024, tn=512):
    m, k = x.shape
    n = w.shape[1]
    tn = min(tn, n)
    assert m % tm == 0 and n % tn == 0
    return pl.pallas_call(
        _norm_matmul_kernel,
        grid=(m // tm, n // tn),
        in_specs=[
            pl.BlockSpec((tm, k), lambda i, j: (i, 0)),
            pl.BlockSpec((1, k), lambda i, j: (0, 0)),
            pl.BlockSpec((k, tn), lambda i, j: (0, j)),
        ],
        out_specs=pl.BlockSpec((tm, tn), lambda i, j: (i, j)),
        out_shape=jax.ShapeDtypeStruct((m, n), F32),
        scratch_shapes=[pltpu.VMEM((tm, k), BF16)],
        compiler_params=pltpu.CompilerParams(
            dimension_semantics=("parallel", "arbitrary"),
            vmem_limit_bytes=VMEM_LIMIT_BYTES),
        name="norm_matmul",
    )(x, g.reshape(1, k), w)


def _matmul_residual_kernel(y_ref, w_ref, r_ref, o_ref):
    o_ref[...] = r_ref[...] + jnp.dot(y_ref[...].astype(BF16), w_ref[...],
                                      preferred_element_type=F32)


def matmul_residual(y, w, res, *, tm=1024, tn=512):
    m, k = y.shape
    n = w.shape[1]
    assert m % tm == 0 and n % tn == 0
    return pl.pallas_call(
        _matmul_residual_kernel,
        grid=(m // tm, n // tn),
        in_specs=[
            pl.BlockSpec((tm, k), lambda i, j: (i, 0)),
            pl.BlockSpec((k, tn), lambda i, j: (0, j)),
            pl.BlockSpec((tm, tn), lambda i, j: (i, j)),
        ],
        out_specs=pl.BlockSpec((tm, tn), lambda i, j: (i, j)),
        out_shape=jax.ShapeDtypeStruct((m, n), F32),
        compiler_params=pltpu.CompilerParams(
            dimension_semantics=("parallel", "arbitrary"),
            vmem_limit_bytes=VMEM_LIMIT_BYTES),
        name="matmul_residual",
    )(y, w, res)


def _rmsnorm_kernel(x_ref, g_ref, o_ref):
    x = x_ref[...]
    ms = jnp.mean(x * x, axis=-1, keepdims=True)
    o_ref[...] = x * lax.rsqrt(ms + EPS) * g_ref[...]


def rmsnorm_rows(x, g, *, tm=512):
    m, k = x.shape
    return pl.pallas_call(
        _rmsnorm_kernel,
        grid=(m // tm,),
        in_specs=[pl.BlockSpec((tm, k), lambda i: (i, 0)),
                  pl.BlockSpec((1, k), lambda i: (0, 0))],
        out_specs=pl.BlockSpec((tm, k), lambda i: (i, 0)),
        out_shape=jax.ShapeDtypeStruct((m, k), F32),
        compiler_params=pltpu.CompilerParams(dimension_semantics=("parallel",)),
        name="rmsnorm",
    )(x, g.reshape(1, k))


NCAND = PEER_TOPK * PEER_TOPK
ENC_SHIFT = PEER_NKEYS * PEER_NKEYS
CAND_AB = [(a, b) for a in range(PEER_TOPK) for b in range(PEER_TOPK) if (a + 1) * (b + 1) <= PEER_TOPK]
NCAND_KEPT = -(-len(CAND_AB) // SUBLANES) * SUBLANES
NSEL_EXT = 2 * PEER_TOPK + SUBLANES
NEG_INF = float("-inf")


def candidate_matrices():
    expand = np.zeros((NCAND_KEPT, NSEL_EXT), np.float32)
    scale = np.zeros((NCAND_KEPT, NSEL_EXT), np.float32)
    for j, (a, b) in enumerate(CAND_AB):
        expand[j, a] = expand[j, PEER_TOPK + b] = 1.0
        scale[j, a], scale[j, PEER_TOPK + b] = PEER_NKEYS, 1.0
        scale[j, 2 * PEER_TOPK] = (a * PEER_TOPK + b) * ENC_SHIFT
    expand[len(CAND_AB):, 2 * PEER_TOPK] = -1e30
    scale[len(CAND_AB):, 2 * PEER_TOPK] = (NCAND - 1) * ENC_SHIFT
    return jnp.asarray(expand, BF16), jnp.asarray(scale, BF16)


def _split2(x):
    p0 = x.astype(BF16)
    return p0, (x - p0.astype(F32)).astype(BF16)


def _split3(x):
    p0 = x.astype(BF16)
    r = x - p0.astype(F32)
    p1 = r.astype(BF16)
    p2 = (r - p1.astype(F32)).astype(BF16)
    return p0, p1, p2


def _peer_topk_kernel(q_ref, k_ref, expand_ref, scale_ref, eidx_ref, gate_ref,
                      s_scr, sv_scr, si_scr, c_scr, enc_scr, top_scr, e_scr):
    tm = q_ref.shape[0]
    half = PEER_DQ // 2
    key_f = lax.broadcasted_iota(jnp.int32, (PEER_NKEYS, tm), 0).astype(F32)
    contract_last = (((1,), (1,)), ((), ()))

    for i in range(2):
        q = q_ref[:, i * half:(i + 1) * half].astype(BF16)
        s_scr[i] = lax.dot_general(k_ref[0, i], q, contract_last, preferred_element_type=F32)

    def round1(r, carry):
        for i in range(2):
            s = s_scr[i]
            m = jnp.max(s, axis=0, keepdims=True)
            pos = jnp.min(jnp.where(s == m, key_f, float(PEER_NKEYS)), axis=0, keepdims=True)
            s_scr[i] = jnp.where(key_f == pos, NEG_INF, s)
            sv_scr[pl.ds(i * PEER_TOPK + r, 1), :] = m
            si_scr[pl.ds(i * PEER_TOPK + r, 1), :] = pos
        return carry

    lax.fori_loop(0, PEER_TOPK, round1, 0)

    ones_row = (lax.broadcasted_iota(jnp.int32, (NSEL_EXT - 2 * PEER_TOPK, tm), 0) == 0).astype(F32)
    sv_scr[2 * PEER_TOPK:, :] = ones_row
    si_scr[2 * PEER_TOPK:, :] = ones_row
    c_scr[...] = sum(jnp.dot(expand_ref[...], p, preferred_element_type=F32) for p in _split3(sv_scr[...]))
    enc_scr[...] = jnp.dot(scale_ref[...], si_scr[...].astype(BF16), preferred_element_type=F32)
    big = float(NCAND * ENC_SHIFT)

    def round2(r, carry):
        c = c_scr[...]
        enc = enc_scr[...]
        m = jnp.max(c, axis=0, keepdims=True)
        e = jnp.min(jnp.where(c == m, enc, big), axis=0, keepdims=True)
        c_scr[...] = jnp.where(enc == e, NEG_INF, c)
        top_scr[pl.ds(r, 1), :] = m
        e_scr[pl.ds(r, 1), :] = e
        return carry

    lax.fori_loop(0, PEER_TOPK, round2, 0)
    top = top_scr[...]
    ex = jnp.exp(top - jnp.max(top, axis=0, keepdims=True))
    gate_ref[0] = ex / jnp.sum(ex, axis=0, keepdims=True)
    eidx_ref[0] = e_scr[...].astype(jnp.int32) & (ENC_SHIFT - 1)


def peer_topk(q, subkeys, *, tm=512):
    T = q.shape[0]
    assert T % tm == 0
    vm = pltpu.VMEM
    eidx, gate = pl.pallas_call(
        _peer_topk_kernel,
        grid=(T // tm, PEER_HEADS),
        in_specs=[pl.BlockSpec((tm, PEER_DQ), lambda i, h: (i, h)),
                  pl.BlockSpec((1, 2, PEER_NKEYS, PEER_DQ // 2), lambda i, h: (h, 0, 0, 0)),
                  pl.BlockSpec((NCAND_KEPT, NSEL_EXT), lambda i, h: (0, 0)),
                  pl.BlockSpec((NCAND_KEPT, NSEL_EXT), lambda i, h: (0, 0))],
        out_specs=[pl.BlockSpec((1, PEER_TOPK, tm), lambda i, h: (h, 0, i)),
                   pl.BlockSpec((1, PEER_TOPK, tm), lambda i, h: (h, 0, i))],
        out_shape=[jax.ShapeDtypeStruct((PEER_HEADS, PEER_TOPK, T), jnp.int32),
                   jax.ShapeDtypeStruct((PEER_HEADS, PEER_TOPK, T), F32)],
        scratch_shapes=[vm((2, PEER_NKEYS, tm), F32), vm((NSEL_EXT, tm), F32), vm((NSEL_EXT, tm), F32),
                        vm((NCAND_KEPT, tm), F32), vm((NCAND_KEPT, tm), F32), vm((PEER_TOPK, tm), F32),
                        vm((PEER_TOPK, tm), F32)],
        compiler_params=pltpu.CompilerParams(dimension_semantics=("parallel", "arbitrary")),
        name="peer_topk",
    )(q, subkeys.astype(BF16), *candidate_matrices())
    fix = lambda a: a.transpose(2, 0, 1).reshape(T, PEER_HEADS * PEER_TOPK)
    return fix(eidx), fix(gate)


def _fold(xa, xb, mask, shift):
    s = jnp.where(mask, xa, xb)
    t = jnp.where(mask, xb, xa)
    return s + pltpu.roll(t, shift, axis=0)


def _unpack(w):
    lo = lax.bitcast_convert_type(w << 16, F32)
    hi = lax.bitcast_convert_type(w & jnp.int32(-65536), F32)
    return lo, hi


def _peer_dot_kernel(meta_ref, *refs, chunk):
    row_refs = refs[:SUBLANES]
    gtok_ref, x_ref, gate_ref, u_ref, a_ref, c_scr = refs[SUBLANES:]
    k = pl.program_id(1)
    gpc = chunk // SUBLANES
    c0 = meta_ref[0, 0, k]
    c1 = meta_ref[0, 0, k + 1]

    @pl.when(k == 0)
    def _():
        a_ref[...] = jnp.zeros_like(a_ref)

    c_scr[...] = jnp.zeros_like(c_scr)
    row_i = lax.broadcasted_iota(jnp.int32, (SUBLANES, LANES), 0)
    m1 = (row_i & 1) == 0
    m2 = (row_i & 2) == 0
    m3 = (row_i & 4) == 0
    ones = jnp.ones((SUBLANES, LANES), BF16)
    contract_last = (((1,), (1,)), ((), ()))

    def finish(c, buf):
        h8 = sum(lax.dot_general(ones, p, contract_last, preferred_element_type=F32) for p in _split2(c_scr[buf]))
        h = h8[0:1, :]
        off = pl.multiple_of(c * chunk, chunk)
        act = 0.5 * h * (1.0 + lax.erf(h * 0.7071067811865476))
        a_ref[0, :, pl.ds(off, chunk)] = act * gate_ref[0, :, pl.ds(off, chunk)]

    def chunk_body(c, carry):
        buf = (c - c0) & 1
        finish(jnp.maximum(c - 1, c0), 1 - buf)
        for gl in range(gpc):
            g = c * gpc + gl
            tok = gtok_ref[0, 0, g]
            xlo = x_ref[tok, 0:P_ROWS, :]
            xhi = x_ref[tok, P_ROWS:D_ROWS, :]
            ps = []
            for i in range(SUBLANES):
                r8 = pl.multiple_of(row_refs[i][0, 0, g], SUBLANES)
                lo, hi = _unpack(u_ref[pl.ds(r8, SUBLANES), :])
                ps.append(lo * xlo + hi * xhi)
            q = [_fold(ps[2 * i], ps[2 * i + 1], m1, 1) for i in range(4)]
            r = [_fold(q[0], q[1], m2, 2), _fold(q[2], q[3], m2, 2)]
            c_scr[buf, gl * SUBLANES:(gl + 1) * SUBLANES, :] = _fold(r[0], r[1], m3, 4)
        return carry

    lax.fori_loop(c0, c1, chunk_body, 0)

    @pl.when(c1 > c0)
    def _():
        finish(c1 - 1, (c1 - 1 - c0) & 1)


def _peer_acc_kernel(meta_ref, *refs, chunk):
    row_refs = refs[:SUBLANES]
    a_refs = refs[SUBLANES:2 * SUBLANES]
    gtok_ref, res_ref, v_ref, o_ref = refs[2 * SUBLANES:]
    k = pl.program_id(1)
    gpc = chunk // SUBLANES

    @pl.when(k == 0)
    def _():
        o_ref[...] = res_ref[...]

    def groups(gp, carry):
        for sub in range(GROUP_UNROLL):
            g = gp * GROUP_UNROLL + sub
            tok = gtok_ref[0, 0, g]
            alo = ahi = None
            for i in range(SUBLANES):
                r8 = pl.multiple_of(row_refs[i][0, 0, g], SUBLANES)
                lo, hi = _unpack(v_ref[pl.ds(r8, SUBLANES), :])
                a = a_refs[i][0, 0, g]
                alo = a * lo if alo is None else alo + a * lo
                ahi = a * hi if ahi is None else ahi + a * hi
            o_ref[tok, 0:P_ROWS, :] += alo
            o_ref[tok, P_ROWS:D_ROWS, :] += ahi
        return carry

    per_chunk = gpc // GROUP_UNROLL
    lax.fori_loop(meta_ref[0, 0, k] * per_chunk, meta_ref[0, 0, k + 1] * per_chunk, groups, 0)


def pack_table(w):
    e = w.shape[0]
    bits = lax.bitcast_convert_type(w.astype(BF16), jnp.uint16).astype(jnp.uint32)
    half = D_MODEL // 2
    packed = (bits[:, half:] << 16) | bits[:, :half]
    return lax.bitcast_convert_type(packed, jnp.int32).reshape(e * P_ROWS, LANES)


def by_lane(a):
    nb, cap = a.shape[0], a.shape[-1]
    a = a.reshape(nb, cap // SUBLANES, SUBLANES)
    return [a[:, :, i].reshape(nb, 1, cap // SUBLANES) for i in range(SUBLANES)]


def peer_route(eidx, gate, *, n_exp, tile, blk, chunk):
    T, S = eidx.shape
    nb = T // blk
    nt = n_exp // tile
    shift = tile.bit_length() - 1
    assert 1 << shift == tile and T % blk == 0 and S + SUBLANES <= 256 and chunk <= 256
    assert blk & (blk - 1) == 0 and blk < 256 and shift <= ROUTE_POS_SHIFT and nt <= 4
    i32 = jnp.int32
    tiles = jnp.arange(nt, dtype=i32)
    tile_id = eidx >> shift
    cnt = jnp.sum(tile_id[:, :, None] == tiles[None, None, :], axis=1, dtype=i32)
    tpad = (-cnt) % SUBLANES
    t_in = (jnp.arange(T, dtype=i32) % blk)[:, None]
    dead = i32(nt << ROUTE_TILE_SHIFT)
    tile_f, tok_f, pos_f = ROUTE_TILE_SHIFT, ROUTE_TOK_SHIFT, ROUTE_POS_SHIFT
    key_real = ((tile_id << tile_f) | (t_in << tok_f) | (jnp.arange(S, dtype=i32)[None, :] << pos_f)
                | (eidx & (tile - 1)))
    i8 = jnp.arange(SUBLANES, dtype=i32)
    key_tpad = jnp.where(i8[None, None, :] < tpad[:, :, None],
                         (tiles[None, :, None] << tile_f) | (t_in[:, :, None] << tok_f)
                         | ((S + i8)[None, None, :] << pos_f), dead)
    n_run = jnp.sum((cnt + tpad).reshape(nb, blk, nt), axis=1)
    bpad = (-n_run) % chunk
    ib = jnp.arange(chunk - SUBLANES, dtype=i32)
    key_bpad = jnp.where(ib[None, None, :] < bpad[:, :, None],
                         (tiles[None, :, None] << tile_f) | i32(blk << tok_f) | (ib[None, None, :] << pos_f), dead)
    n_real, n_tp, n_bp = blk * S, blk * nt * SUBLANES, nt * (chunk - SUBLANES)
    cap = -(-(n_real + n_tp + nt * chunk) // 1024) * 1024
    fill = cap - (n_real + n_tp + n_bp)
    keys = jnp.concatenate([key_real.reshape(nb, n_real), key_tpad.reshape(nb, n_tp), key_bpad.reshape(nb, n_bp),
                            jnp.full((nb, fill), dead, i32)], axis=1)
    gates = jnp.concatenate([gate.reshape(nb, n_real), jnp.zeros((nb, cap - n_real), gate.dtype)], axis=1)
    keys, gates = lax.sort((keys, gates), dimension=1, num_keys=1)
    rows = (keys & ((1 << pos_f) - 1)) * SUBLANES
    gtok = (keys[:, ::SUBLANES] >> tok_f) & (blk - 1)
    n_chunks = (n_run + bpad) // chunk
    cstart = jnp.concatenate([jnp.zeros((nb, 1), i32), jnp.cumsum(n_chunks, axis=1)], axis=1)
    meta = jnp.pad(cstart, ((0, 0), (0, LANES - nt - 1)))
    r3 = lambda a: a.reshape(nb, 1, a.shape[1])
    return r3(meta), by_lane(rows), r3(gtok), r3(gates), cap


def peer_experts(h, res, eidx, gate, u_packed, v_packed, *, tile, blk):
    T = h.shape[0]
    chunk = PEER_CHUNK_SLOTS
    n_exp = u_packed.shape[0] // P_ROWS
    nt = n_exp // tile
    nb = T // blk
    meta, rows, gtok, gates, cap = peer_route(eidx, gate, n_exp=n_exp, tile=tile, blk=blk, chunk=chunk)
    tok_spec = pl.BlockSpec((blk, D_ROWS, LANES), lambda j, k: (j, 0, 0))
    tab_spec = pl.BlockSpec((tile * P_ROWS, LANES), lambda j, k: (k, 0))
    smem = lambda n: pl.BlockSpec((1, 1, n), lambda j, k: (j, 0, 0), memory_space=pltpu.SMEM,
                                  pipeline_mode=pl.Buffered(1))
    slot_spec = pl.BlockSpec((1, 1, cap), lambda j, k: (j, 0, 0))
    params = pltpu.CompilerParams(dimension_semantics=("parallel", "arbitrary"), vmem_limit_bytes=PEER_VMEM_LIMIT)
    a = pl.pallas_call(
        functools.partial(_peer_dot_kernel, chunk=chunk),
        grid=(nb, nt),
        in_specs=[smem(LANES)] + [smem(cap // SUBLANES)] * (SUBLANES + 1) + [tok_spec, slot_spec, tab_spec],
        out_specs=slot_spec,
        out_shape=jax.ShapeDtypeStruct((nb, 1, cap), F32),
        scratch_shapes=[pltpu.VMEM((2, chunk, LANES), F32)],
        compiler_params=params, name="peer_dot",
    )(meta, *rows, gtok, h.reshape(T, D_ROWS, LANES), gates, u_packed)
    out = pl.pallas_call(
        functools.partial(_peer_acc_kernel, chunk=chunk),
        grid=(nb, nt),
        in_specs=[smem(LANES)] + [smem(cap // SUBLANES)] * (2 * SUBLANES + 1) + [tok_spec, tab_spec],
        out_specs=tok_spec,
        out_shape=jax.ShapeDtypeStruct((T, D_ROWS, LANES), F32),
        compiler_params=params, name="peer_acc",
    )(meta, *rows, *by_lane(a.reshape(nb, cap)), gtok, res.reshape(T, D_ROWS, LANES), v_packed)
    return out.reshape(T, D_MODEL)


ATT_W = 3 * N_DIL * D_ATT
ROPE_HALF = ROPE_DIM // 2
ATT_TQ = 128
N_QK_BLOCKS = 2 * N_DIL


def rope_lane_tables(length):
    inv = ROPE_THETA ** (-jnp.arange(0, ROPE_DIM, 2, dtype=F32) / ROPE_DIM)
    ang = jnp.arange(length, dtype=F32)[:, None] * inv[None, :]
    cos, sin = jnp.cos(ang), jnp.sin(ang)
    lane = np.arange(LANES) % HEAD_DIM
    j = lane % ROPE_HALF
    first, second = lane < ROPE_HALF, (lane >= ROPE_HALF) & (lane < ROPE_DIM)
    c = jnp.where((first | second)[None, :], cos[:, j], 1.0)
    s1 = jnp.where(second[None, :], sin[:, j], 0.0)
    s2 = jnp.where(first[None, :], -sin[:, j], 0.0)
    return c, s1, s2


def _rope_cast_kernel(x_ref, c_ref, s1_ref, s2_ref, o_ref, *, q_scale):
    col = pl.program_id(1)

    @pl.when(col < N_QK_BLOCKS)
    def _():
        scale = jnp.where(col < N_DIL, q_scale, 1.0)
        c, s1, s2 = c_ref[...], s1_ref[...], s2_ref[...]
        for t in range(D_ATT // LANES):
            x = x_ref[:, t * LANES:(t + 1) * LANES]
            y = x * c + pltpu.roll(x, ROPE_HALF, axis=1) * s1 + pltpu.roll(x, LANES - ROPE_HALF, axis=1) * s2
            o_ref[:, t * LANES:(t + 1) * LANES] = (y * scale).astype(BF16)

    @pl.when(col >= N_QK_BLOCKS)
    def _():
        o_ref[...] = x_ref[...].astype(BF16)


def rope_cast(att, seq_len, *, tm=512):
    T = att.shape[0]
    assert seq_len % tm == 0
    c, s1, s2 = rope_lane_tables(seq_len)
    nblk = seq_len // tm
    tab = pl.BlockSpec((tm, LANES), lambda i, j: (i % nblk, 0))
    blk = pl.BlockSpec((tm, D_ATT), lambda i, j: (i, j))
    return pl.pallas_call(
        functools.partial(_rope_cast_kernel, q_scale=HEAD_DIM ** -0.5),
        grid=(T // tm, ATT_W // D_ATT),
        in_specs=[blk, tab, tab, tab],
        out_specs=blk,
        out_shape=jax.ShapeDtypeStruct((T, ATT_W), BF16),
        compiler_params=pltpu.CompilerParams(dimension_semantics=("parallel", "parallel")),
        name="rope_cast",
    )(att, c, s1, s2)


def _band_attn_kernel(q_ref, kp_ref, kc_ref, kn_ref, vp_ref, vc_ref, vn_ref, o_ref, lse_ref, s_scr, p_scr,
                      *, hw, sub_len):
    i = pl.program_id(2)
    tq = q_ref.shape[0]
    n_heads = D_ATT // HEAD_DIM
    q_pos = i * tq + lax.broadcasted_iota(jnp.int32, (tq, 3 * tq), 0)
    k_pos = (i - 1) * tq + lax.broadcasted_iota(jnp.int32, (tq, 3 * tq), 1)
    valid = (jnp.abs(k_pos - q_pos) <= hw) & (k_pos >= 0) & (k_pos < sub_len)
    low = lax.broadcasted_iota(jnp.int32, (tq, LANES), 1) < HEAD_DIM
    contract_last = (((1,), (1,)), ((), ()))
    k_refs = (kp_ref, kc_ref, kn_ref)
    v_refs = (vp_ref, vc_ref, vn_ref)
    for h in range(n_heads):
        sl = slice(h // 2 * LANES, (h // 2 + 1) * LANES)
        q2 = q_ref[:, sl]
        qh = jnp.where(low if h % 2 == 0 else ~low, q2, jnp.zeros_like(q2))
        for j in range(3):
            s_scr[h, :, j * tq:(j + 1) * tq] = lax.dot_general(qh, k_refs[j][:, sl], contract_last,
                                                                 preferred_element_type=F32)
    s = jnp.where(valid[None], s_scr[...], NEG)
    m = jnp.max(s, axis=-1, keepdims=True)
    p = jnp.exp(s - m)
    l = jnp.sum(p, axis=-1, keepdims=True)
    p_scr[...] = p.astype(BF16)
    lse = m + jnp.log(l)
    inv = 1.0 / l
    for pair in range(n_heads // 2):
        sl = slice(pair * LANES, (pair + 1) * LANES)
        os = []
        for h in (2 * pair, 2 * pair + 1):
            o = sum(jnp.dot(p_scr[h, :, j * tq:(j + 1) * tq], v_refs[j][:, sl], preferred_element_type=F32)
                    for j in range(3))
            os.append(o * inv[h])
        o_ref[:, sl] = jnp.where(low, os[0], os[1])
        lse_ref[:, sl] = jnp.where(low, lse[2 * pair], lse[2 * pair + 1])


def band_attention(qkv, g, batch, seq_len, *, tq=ATT_TQ):
    win, dil = DILATION_CFG[g]
    hw = win // (2 * dil)
    sub_len = seq_len // dil
    assert sub_len % tq == 0 and hw <= tq
    nq = sub_len // tq
    ncol = ATT_W // D_ATT
    x = qkv.reshape(batch, sub_len, dil * ATT_W)

    def spec(kind, shift):
        def index(b, r, i):
            return (b, jnp.clip(i + shift, 0, nq - 1), r * ncol + kind * N_DIL + g)
        return pl.BlockSpec((None, tq, D_ATT), index)

    out_spec = pl.BlockSpec((None, tq, D_ATT), lambda b, r, i: (b, i, r))
    o, lse = pl.pallas_call(
        functools.partial(_band_attn_kernel, hw=hw, sub_len=sub_len),
        grid=(batch, dil, nq),
        in_specs=[spec(0, 0), spec(1, -1), spec(1, 0), spec(1, 1), spec(2, -1), spec(2, 0), spec(2, 1)],
        out_specs=[out_spec, out_spec],
        out_shape=[jax.ShapeDtypeStruct((batch, sub_len, dil * D_ATT), F32)] * 2,
        scratch_shapes=[pltpu.VMEM((D_ATT // HEAD_DIM, tq, 3 * tq), F32),
                        pltpu.VMEM((D_ATT // HEAD_DIM, tq, 3 * tq), BF16)],
        compiler_params=pltpu.CompilerParams(dimension_semantics=("parallel", "parallel", "parallel"),
                                             vmem_limit_bytes=VMEM_LIMIT_BYTES),
        name=f"band_attn_d{dil}",
    )(x, x, x, x, x, x, x)
    return o.reshape(batch * seq_len, D_ATT), lse.reshape(batch * seq_len, D_ATT)


def _attn_merge_kernel(o0, o1, o2, l0, l1, l2, g_ref, y_ref):
    ls = [l0[...], l1[...], l2[...]]
    m = jnp.maximum(jnp.maximum(ls[0], ls[1]), ls[2])
    es = [jnp.exp(l - m) for l in ls]
    den = es[0] + es[1] + es[2]
    o = (es[0] * o0[...] + es[1] * o1[...] + es[2] * o2[...]) / den
    ms = jnp.mean(o * o, axis=-1, keepdims=True)
    y_ref[...] = o * lax.rsqrt(ms + EPS) * g_ref[...]


def attn_merge(outs, lses, norm_w, *, tm=512):
    T = outs[0].shape[0]
    blk = pl.BlockSpec((tm, D_ATT), lambda i: (i, 0))
    return pl.pallas_call(
        _attn_merge_kernel,
        grid=(T // tm,),
        in_specs=[blk] * 6 + [pl.BlockSpec((1, D_ATT), lambda i: (0, 0))],
        out_specs=blk,
        out_shape=jax.ShapeDtypeStruct((T, D_ATT), F32),
        compiler_params=pltpu.CompilerParams(dimension_semantics=("parallel",)),
        name="attn_merge",
    )(*outs, *lses, norm_w.reshape(1, D_ATT))


def attention_mixer(att, norm_w, batch, seq_len):
    qkv = rope_cast(att, seq_len)
    res = [band_attention(qkv, g, batch, seq_len) for g in range(N_DIL)]
    return attn_merge([r[0] for r in res], [r[1] for r in res], norm_w)


CONV_HALO = SUBLANES


def _conv_taps(x, w_ref, pad_scr, width):
    n = x.shape[0]
    half = width // 2
    zeros = jnp.zeros((CONV_HALO, LANES), F32)
    pad_scr[0:CONV_HALO, :] = zeros
    pad_scr[CONV_HALO + n:CONV_HALO + n + CONV_HALO, :] = zeros
    pad_scr[CONV_HALO:CONV_HALO + n, :] = x
    acc = jnp.zeros((n, LANES), F32)
    for k in range(width):
        acc = acc + pad_scr[CONV_HALO - half + k:CONV_HALO - half + k + n, :] * w_ref[k:k + 1, :]
    return acc


def _silu(x):
    return x * (1.0 / (1.0 + jnp.exp(-x)))


def _ssd_conv_kernel(x_ref, w_ref, b_ref, o_ref, pad_scr):
    o_ref[...] = _silu(_conv_taps(x_ref[...], w_ref, pad_scr, SSD_CONV) + b_ref[...])


def _shortconv_kernel(bg_ref, cg_ref, hx_ref, w_ref, o_ref, pad_scr):
    o_ref[...] = bg_ref[...] * _conv_taps(cg_ref[...] * hx_ref[...], w_ref, pad_scr, SC_CONV)


def _strip_call(kernel, name, n_out, batch, seq_len, in_specs, args):
    return pl.pallas_call(
        kernel,
        grid=(batch, n_out // LANES),
        in_specs=in_specs,
        out_specs=pl.BlockSpec((seq_len, LANES), lambda b, c: (b, c)),
        out_shape=jax.ShapeDtypeStruct((batch * seq_len, n_out), F32),
        scratch_shapes=[pltpu.VMEM((seq_len + 2 * CONV_HALO, LANES), F32)],
        compiler_params=pltpu.CompilerParams(dimension_semantics=("parallel", "parallel"),
                                             vmem_limit_bytes=VMEM_LIMIT_BYTES),
        name=name,
    )(*args)


def ssd_conv(xbc, w, bias, batch, seq_len):
    strip = lambda off: pl.BlockSpec((seq_len, LANES), lambda b, c: (b, c + off))
    taps = pl.BlockSpec((SSD_CONV, LANES), lambda b, c: (0, c))
    vec = pl.BlockSpec((1, LANES), lambda b, c: (0, c))
    return _strip_call(_ssd_conv_kernel, "ssd_conv", XBC, batch, seq_len, [strip(0), taps, vec],
                       (xbc, w, bias.reshape(1, XBC)))


def shortconv(sc, w, batch, seq_len):
    nblk = D_SC // LANES
    strip = lambda off: pl.BlockSpec((seq_len, LANES), lambda b, c: (b, c + off))
    taps = pl.BlockSpec((SC_CONV, LANES), lambda b, c: (0, c))
    return _strip_call(_shortconv_kernel, "shortconv", D_SC, batch, seq_len,
                       [strip(0), strip(nblk), strip(2 * nblk), taps], (sc, sc, sc, w))


HEAD_PAIRS = D_SSD // LANES
PAIRS_PER_GROUP = HEAD_PAIRS // SSD_GROUPS


def _dot3(x, w):
    return sum(jnp.dot(p, w, preferred_element_type=F32) for p in _split3(x))


def _softplus(x):
    return jnp.maximum(x, 0.0) + jnp.log(1.0 + jnp.exp(-jnp.abs(x)))


def _ssd_scan_kernel(xbc_ref, dt_ref, dtt_ref, bias_ref, a_ref, biast_ref, at_ref, erep_ref, eexp_ref,
                     y_ref, h_scr, cumrep_scr):
    d = pl.program_id(1)
    c = pl.program_id(2)
    q = SSD_CHUNK

    @pl.when(c == 0)
    def _():
        h_scr[...] = jnp.zeros_like(h_scr)

    row = lax.broadcasted_iota(jnp.int32, (q, q), 0)
    col = lax.broadcasted_iota(jnp.int32, (q, q), 1)
    sign = jnp.where(d == 0, 1, -1)
    mask = (row - col) * sign >= 0
    mask01 = jnp.where(mask, 1.0, 0.0).astype(BF16)
    low = lax.broadcasted_iota(jnp.int32, (q, LANES), 1) < HEAD_DIM
    contract_last = (((1,), (1,)), ((), ()))
    contract_first = (((0,), (0,)), ((), ()))

    dt = _softplus(dt_ref[...] + bias_ref[...])
    a = dt * a_ref[...]
    cum = sum(jnp.dot(mask01, p, preferred_element_type=F32) for p in _split3(a))
    eexp = eexp_ref[0]
    cumrep_scr[...] = _dot3(cum, erep_ref[0])
    cum_exp = _dot3(cum, eexp)
    dt_exp = _dot3(dt, eexp)
    total_exp = jnp.sum(_dot3(a, eexp), axis=0, keepdims=True)

    heads = pl.ds(pl.multiple_of(d * SSD_HEADS, SUBLANES), SSD_HEADS)
    at = _softplus(dtt_ref[heads, :] + biast_ref[heads, :]) * at_ref[heads, :]
    cum_t = sum(lax.dot_general(p, mask01, contract_last, preferred_element_type=F32) for p in _split3(at))

    for g in range(SSD_GROUPS):
        b_g = xbc_ref[:, D_SSD + g * SSD_STATE:D_SSD + (g + 1) * SSD_STATE].astype(BF16)
        c_g = xbc_ref[:, D_SSD + (SSD_GROUPS + g) * SSD_STATE:D_SSD + (SSD_GROUPS + g + 1) * SSD_STATE].astype(BF16)
        cb = lax.dot_general(c_g, b_g, contract_last, preferred_element_type=F32)
        for pp in range(PAIRS_PER_GROUP):
            p = g * PAIRS_PER_GROUP + pp
            sl = slice(p * LANES, (p + 1) * LANES)
            xdt = xbc_ref[:, sl] * dt_exp[:, sl]
            xdt_b = xdt.astype(BF16)
            yd = []
            for hh in range(2):
                h = 2 * p + hh
                seg = cumrep_scr[:, h * LANES:(h + 1) * LANES] - cum_t[h:h + 1, :]
                m = (cb * jnp.exp(jnp.where(mask, seg, NEG_INF))).astype(BF16)
                yd.append(jnp.dot(m, xdt_b, preferred_element_type=F32))
            cum_p = cum_exp[:, sl]
            h_in = h_scr[p]
            y_off = jnp.dot(c_g, h_in.astype(BF16), preferred_element_type=F32) * jnp.exp(cum_p)
            y_ref[0, :, sl] = jnp.where(low, yd[0], yd[1]) + y_off
            tot_p = total_exp[:, sl]
            xs_end = (xdt * jnp.exp(tot_p - cum_p)).astype(BF16)
            states = lax.dot_general(b_g, xs_end, contract_first, preferred_element_type=F32)
            h_scr[p] = jnp.exp(tot_p) * h_in + states


def head_expanders():
    erep = np.zeros((2, LANES, SSD_HEADS * LANES), np.float32)
    eexp = np.zeros((2, LANES, D_SSD), np.float32)
    for d in range(2):
        for h in range(SSD_HEADS):
            erep[d, d * SSD_HEADS + h, h * LANES:(h + 1) * LANES] = 1.0
            eexp[d, d * SSD_HEADS + h, h * HEAD_DIM:(h + 1) * HEAD_DIM] = 1.0
    return jnp.asarray(erep, BF16), jnp.asarray(eexp, BF16)


def ssd_scan(xbc_c, dt_raw, dt_bias, a_log, batch, seq_len):
    T = xbc_c.shape[0]
    assert seq_len % SSD_CHUNK == 0
    nc = seq_len // SSD_CHUNK
    lane_pad = LANES - 2 * SSD_HEADS
    dt_l = jnp.pad(dt_raw, ((0, 0), (0, lane_pad)))
    dt_t = dt_raw.T
    bias = dt_bias.reshape(2 * SSD_HEADS).astype(F32)
    a_neg = -jnp.exp(a_log.reshape(2 * SSD_HEADS).astype(F32))
    erep, eexp = head_expanders()

    def chunk(b, d, c):
        return b * nc + jnp.where(d == 0, c, nc - 1 - c)

    vec = pl.BlockSpec((1, LANES), lambda b, d, c: (0, 0))
    colv = pl.BlockSpec((2 * SSD_HEADS, 1), lambda b, d, c: (0, 0))
    return pl.pallas_call(
        _ssd_scan_kernel,
        grid=(batch, 2, nc),
        in_specs=[pl.BlockSpec((SSD_CHUNK, XBC), lambda b, d, c: (chunk(b, d, c), 0)),
                  pl.BlockSpec((SSD_CHUNK, LANES), lambda b, d, c: (chunk(b, d, c), 0)),
                  pl.BlockSpec((2 * SSD_HEADS, SSD_CHUNK), lambda b, d, c: (0, chunk(b, d, c))),
                  vec, vec, colv, colv,
                  pl.BlockSpec((1, LANES, SSD_HEADS * LANES), lambda b, d, c: (d, 0, 0)),
                  pl.BlockSpec((1, LANES, D_SSD), lambda b, d, c: (d, 0, 0))],
        out_specs=pl.BlockSpec((1, SSD_CHUNK, D_SSD), lambda b, d, c: (d, chunk(b, d, c), 0)),
        out_shape=jax.ShapeDtypeStruct((2, T, D_SSD), F32),
        scratch_shapes=[pltpu.VMEM((HEAD_PAIRS, SSD_STATE, LANES), F32),
                        pltpu.VMEM((SSD_CHUNK, SSD_HEADS * LANES), F32)],
        compiler_params=pltpu.CompilerParams(dimension_semantics=("parallel", "arbitrary", "arbitrary"),
                                             vmem_limit_bytes=VMEM_LIMIT_BYTES),
        name="ssd_scan",
    )(xbc_c, dt_l, dt_t, jnp.pad(bias, (0, lane_pad)).reshape(1, LANES),
      jnp.pad(a_neg, (0, lane_pad)).reshape(1, LANES), bias.reshape(2 * SSD_HEADS, 1),
      a_neg.reshape(2 * SSD_HEADS, 1), erep, eexp)


def _ssd_gate_norm_kernel(y_ref, xs_ref, z_ref, dskip_ref, nw_ref, o_ref):
    y = (y_ref[0] + y_ref[1] + xs_ref[...] * dskip_ref[...]) * _silu(z_ref[...])
    gw = D_SSD // SSD_GROUPS
    for g in range(SSD_GROUPS):
        yg = y[:, g * gw:(g + 1) * gw]
        ms = jnp.mean(yg * yg, axis=-1, keepdims=True)
        o_ref[:, g * gw:(g + 1) * gw] = yg * lax.rsqrt(ms + EPS) * nw_ref[:, g * gw:(g + 1) * gw]


def ssd_gate_norm(y2, xbc_c, z, d_skip, norm_w, *, tm=512):
    T = z.shape[0]
    dvec = jnp.repeat(d_skip.astype(F32), HEAD_DIM).reshape(1, D_SSD)
    blk = pl.BlockSpec((tm, D_SSD), lambda i: (i, 0))
    vec = pl.BlockSpec((1, D_SSD), lambda i: (0, 0))
    return pl.pallas_call(
        _ssd_gate_norm_kernel,
        grid=(T // tm,),
        in_specs=[pl.BlockSpec((2, tm, D_SSD), lambda i: (0, i, 0)), blk, blk, vec, vec],
        out_specs=blk,
        out_shape=jax.ShapeDtypeStruct((T, D_SSD), F32),
        compiler_params=pltpu.CompilerParams(dimension_semantics=("parallel",), vmem_limit_bytes=VMEM_LIMIT_BYTES),
        name="ssd_gate_norm",
    )(y2, xbc_c, z, dvec, norm_w.reshape(1, D_SSD))


def ssd_mixer(z, xbc, dt_raw, conv_w, conv_b, dt_bias, a_log, d_skip, norm_w, batch, seq_len):
    xbc_c = ssd_conv(xbc, conv_w, conv_b, batch, seq_len)
    y2 = ssd_scan(xbc_c, dt_raw, dt_bias, a_log, batch, seq_len)
    return ssd_gate_norm(y2, xbc_c, z, d_skip, norm_w)


def trunk(x, norm_mix, w_in, ssd_conv_w, ssd_conv_b, ssd_dt_bias, ssd_a_log, ssd_d, ssd_norm,
          sc_conv_w, sc_norm, att_norm, w_out, norm_ffn, peer_wq, peer_subkeys, peer_u, peer_v, norm_final):
    b, L, D = x.shape
    T = b * L
    x = x.reshape(T, D)
    o_xbc = D_SSD
    o_dt = o_xbc + XBC
    o_sc = o_dt + 2 * SSD_HEADS
    o_att = o_sc + 3 * D_SC
    for l in range(DEPTH):
        w = w_in[l]
        g = norm_mix[l]
        w_dt = jnp.pad(w[:, o_dt:o_sc], ((0, 0), (0, DT_PAD - 2 * SSD_HEADS)))
        z = norm_matmul(x, g, w[:, :o_xbc].astype(BF16), tn=768)
        xbc = norm_matmul(x, g, w[:, o_xbc:o_dt].astype(BF16), tn=640)
        dt_raw = norm_matmul(x, g, w_dt.astype(BF16))[:, :2 * SSD_HEADS]
        sc = norm_matmul(x, g, w[:, o_sc:o_att].astype(BF16), tn=768)
        att = norm_matmul(x, g, w[:, o_att:].astype(BF16), tn=768)
        y = jnp.concatenate([
            ssd_mixer(z, xbc, dt_raw, ssd_conv_w[l], ssd_conv_b[l], ssd_dt_bias[l], ssd_a_log[l], ssd_d[l],
                      ssd_norm[l], b, L),
            rmsnorm_rows(shortconv(sc, sc_conv_w[l], b, L), sc_norm[l]),
            attention_mixer(att, att_norm[l], b, L)], axis=-1)
        x = matmul_residual(y, w_out[l].astype(BF16), x)
        q = norm_matmul(x, norm_ffn[l], peer_wq[l].astype(BF16), tn=1024)
        h = rmsnorm_rows(x, norm_ffn[l])
        eidx, gate = peer_topk(q, peer_subkeys[l])
        x = peer_experts(h, x, eidx, gate, pack_table(peer_u[l]), pack_table(peer_v[l]),
                         tile=PEER_TILE, blk=PEER_BLK)
    return rmsnorm_rows(x, norm_final).reshape(b, L, D)


def kernel(x_prompt, x_sample, norm_mix, w_in, ssd_conv_w, ssd_conv_b, ssd_dt_bias, ssd_a_log, ssd_d, ssd_norm,
           sc_conv_w, sc_norm, att_norm, w_out, norm_ffn, peer_wq, peer_subkeys, peer_u, peer_v, norm_final):
    nb = x_prompt.shape[0]
    x = jnp.concatenate([x_prompt, x_sample], axis=0)
    y = trunk(x, norm_mix, w_in, ssd_conv_w, ssd_conv_b, ssd_dt_bias, ssd_a_log, ssd_d, ssd_norm,
              sc_conv_w, sc_norm, att_norm, w_out, norm_ffn, peer_wq, peer_subkeys, peer_u, peer_v, norm_final)
    return (y[:nb], y[nb:])
```

```python
import functools

import jax
import jax.numpy as jnp
import numpy as np
from jax import lax
from jax.experimental import pallas as pl
from jax.experimental.pallas import tpu as pltpu

D_MODEL = 2048
DEPTH = 2
D_MIX = 3 * D_MODEL // 2
HEAD_DIM = 64
D_SSD = D_MIX // 2
SSD_HEADS = D_SSD // HEAD_DIM
SSD_GROUPS = 4
SSD_STATE = 128
SSD_CONV = 5
SSD_CHUNK = 128
XBC = D_SSD + 2 * SSD_GROUPS * SSD_STATE
D_SC = D_MIX // 4
SC_CONV = 3
D_ATT = D_MIX // 4
DILATION_CFG = ((128, 1), (512, 4), (2048, 16))
N_DIL = len(DILATION_CFG)
ROPE_DIM = HEAD_DIM // 4
ROPE_THETA = 500000.0
PEER_HEADS = 8
PEER_NKEYS = 128
PEER_TOPK = 16
PEER_DQ = 256
EPS = 1e-6
NEG = -1e30
DT_PAD = 128

LANES = 128
SUBLANES = 8
D_ROWS = D_MODEL // LANES
P_ROWS = D_ROWS // 2
PEER_TILE = 4096
PEER_BLK = 128
PEER_CHUNK_SLOTS = 256
GROUP_UNROLL = 8
ROUTE_POS_SHIFT = 12
ROUTE_TOK_SHIFT = 20
ROUTE_TILE_SHIFT = 28

VMEM_LIMIT_BYTES = 56 * 1024 * 1024
PEER_VMEM_LIMIT = VMEM_LIMIT_BYTES

F32 = jnp.float32
BF16 = jnp.bfloat16


def _norm_matmul_kernel(x_ref, g_ref, w_ref, o_ref, h_scr):
    @pl.when(pl.program_id(1) == 0)
    def _():
        x = x_ref[...]
        ms = jnp.mean(x * x, axis=-1, keepdims=True)
        h_scr[...] = (x * lax.rsqrt(ms + EPS) * g_ref[...]).astype(BF16)

    o_ref[...] = jnp.dot(h_scr[...], w_ref[...], preferred_element_type=F32)


def norm_matmul(x, g, w, *, tm=1024, tn=512):
    m, k = x.shape
    n = w.shape[1]
    tn = min(tn, n)
    assert m % tm == 0 and n % tn == 0
    return pl.pallas_call(
        _norm_matmul_kernel,
        grid=(m // tm, n // tn),
        in_specs=[
            pl.BlockSpec((tm, k), lambda i, j: (i, 0)),
            pl.BlockSpec((1, k), lambda i, j: (0, 0)),
            pl.BlockSpec((k, tn), lambda i, j: (0, j)),
        ],
        out_specs=pl.BlockSpec((tm, tn), lambda i, j: (i, j)),
        out_shape=jax.ShapeDtypeStruct((m, n), F32),
        scratch_shapes=[pltpu.VMEM((tm, k), BF16)],
        compiler_params=pltpu.CompilerParams(
            dimension_semantics=("parallel", "arbitrary"),
            vmem_limit_bytes=VMEM_LIMIT_BYTES),
        name="norm_matmul",
    )(x, g.reshape(1, k), w)


def _norm_matmul_emit_kernel(x_ref, g_ref, w_ref, o_ref, h_ref, h_scr):
    @pl.when(pl.program_id(1) == 0)
    def _():
        x = x_ref[...]
        ms = jnp.mean(x * x, axis=-1, keepdims=True)
        h = x * lax.rsqrt(ms + EPS) * g_ref[...]
        h_ref[...] = h
        h_scr[...] = h.astype(BF16)

    o_ref[...] = jnp.dot(h_scr[...], w_ref[...], preferred_element_type=F32)


def norm_matmul_emit(x, g, w, *, tm=512, tn=1024):
    m, k = x.shape
    n = w.shape[1]
    assert m % tm == 0 and n % tn == 0
    return pl.pallas_call(
        _norm_matmul_emit_kernel,
        grid=(m // tm, n // tn),
        in_specs=[
            pl.BlockSpec((tm, k), lambda i, j: (i, 0)),
            pl.BlockSpec((1, k), lambda i, j: (0, 0)),
            pl.BlockSpec((k, tn), lambda i, j: (0, j)),
        ],
        out_specs=[pl.BlockSpec((tm, tn), lambda i, j: (i, j)), pl.BlockSpec((tm, k), lambda i, j: (i, 0))],
        out_shape=[jax.ShapeDtypeStruct((m, n), F32), jax.ShapeDtypeStruct((m, k), F32)],
        scratch_shapes=[pltpu.VMEM((tm, k), BF16)],
        compiler_params=pltpu.CompilerParams(
            dimension_semantics=("parallel", "arbitrary"),
            vmem_limit_bytes=VMEM_LIMIT_BYTES),
        name="norm_matmul_emit",
    )(x, g.reshape(1, k), w)


def _matmul_residual_kernel(*refs, widths):
    y_refs, (w_ref, r_ref, o_ref) = refs[:len(widths)], refs[len(widths):]
    acc = r_ref[...]
    row = 0
    for y_ref, width in zip(y_refs, widths):
        acc = acc + jnp.dot(y_ref[...].astype(BF16), w_ref[row:row + width, :], preferred_element_type=F32)
        row += width
    o_ref[...] = acc


def matmul_residual(ys, w, res, *, tm=1024, tn=512):
    m = res.shape[0]
    widths = tuple(y.shape[1] for y in ys)
    k, n = w.shape
    assert m % tm == 0 and n % tn == 0 and sum(widths) == k
    return pl.pallas_call(
        functools.partial(_matmul_residual_kernel, widths=widths),
        grid=(m // tm, n // tn),
        in_specs=[pl.BlockSpec((tm, width), lambda i, j: (i, 0)) for width in widths] + [
            pl.BlockSpec((k, tn), lambda i, j: (0, j)),
            pl.BlockSpec((tm, tn), lambda i, j: (i, j)),
        ],
        out_specs=pl.BlockSpec((tm, tn), lambda i, j: (i, j)),
        out_shape=jax.ShapeDtypeStruct((m, n), F32),
        compiler_params=pltpu.CompilerParams(
            dimension_semantics=("parallel", "arbitrary"),
            vmem_limit_bytes=VMEM_LIMIT_BYTES),
        name="matmul_residual",
    )(*ys, w, res)


def _rmsnorm_kernel(x_ref, g_ref, o_ref):
    x = x_ref[...]
    ms = jnp.mean(x * x, axis=-1, keepdims=True)
    o_ref[...] = x * lax.rsqrt(ms + EPS) * g_ref[...]


def rmsnorm_rows(x, g, *, tm=512):
    m, k = x.shape
    return pl.pallas_call(
        _rmsnorm_kernel,
        grid=(m // tm,),
        in_specs=[pl.BlockSpec((tm, k), lambda i: (i, 0)),
                  pl.BlockSpec((1, k), lambda i: (0, 0))],
        out_specs=pl.BlockSpec((tm, k), lambda i: (i, 0)),
        out_shape=jax.ShapeDtypeStruct((m, k), F32),
        compiler_params=pltpu.CompilerParams(dimension_semantics=("parallel",)),
        name="rmsnorm",
    )(x, g.reshape(1, k))


NCAND = PEER_TOPK * PEER_TOPK
ENC_SHIFT = PEER_NKEYS * PEER_NKEYS
CAND_AB = [(a, b) for a in range(PEER_TOPK) for b in range(PEER_TOPK) if (a + 1) * (b + 1) <= PEER_TOPK]
NCAND_KEPT = -(-len(CAND_AB) // SUBLANES) * SUBLANES
NSEL_EXT = 2 * PEER_TOPK + SUBLANES
NEG_INF = float("-inf")


def candidate_matrices():
    expand = np.zeros((NCAND_KEPT, NSEL_EXT), np.float32)
    scale = np.zeros((NCAND_KEPT, NSEL_EXT), np.float32)
    for j, (a, b) in enumerate(CAND_AB):
        expand[j, a] = expand[j, PEER_TOPK + b] = 1.0
        scale[j, a], scale[j, PEER_TOPK + b] = PEER_NKEYS, 1.0
        scale[j, 2 * PEER_TOPK] = (a * PEER_TOPK + b) * ENC_SHIFT
    expand[len(CAND_AB):, 2 * PEER_TOPK] = -1e30
    scale[len(CAND_AB):, 2 * PEER_TOPK] = (NCAND - 1) * ENC_SHIFT
    return jnp.asarray(expand, BF16), jnp.asarray(scale, BF16)


def _split2(x):
    p0 = x.astype(BF16)
    return p0, (x - p0.astype(F32)).astype(BF16)


def _split3(x):
    p0 = x.astype(BF16)
    r = x - p0.astype(F32)
    p1 = r.astype(BF16)
    p2 = (r - p1.astype(F32)).astype(BF16)
    return p0, p1, p2


def _peer_topk_kernel(q_ref, k_ref, expand_ref, scale_ref, eidx_ref, gate_ref,
                      s_scr, sv_scr, si_scr, c_scr, enc_scr, top_scr, e_scr):
    tm = q_ref.shape[0]
    half = PEER_DQ // 2
    key_f = lax.broadcasted_iota(jnp.int32, (PEER_NKEYS, tm), 0).astype(F32)
    contract_last = (((1,), (1,)), ((), ()))

    for i in range(2):
        q = q_ref[:, i * half:(i + 1) * half].astype(BF16)
        s_scr[i] = lax.dot_general(k_ref[0, i], q, contract_last, preferred_element_type=F32)

    def round1(r, carry):
        for i in range(2):
            s = s_scr[i]
            m = jnp.max(s, axis=0, keepdims=True)
            pos = jnp.min(jnp.where(s == m, key_f, float(PEER_NKEYS)), axis=0, keepdims=True)
            s_scr[i] = jnp.where(key_f == pos, NEG_INF, s)
            sv_scr[pl.ds(i * PEER_TOPK + r, 1), :] = m
            si_scr[pl.ds(i * PEER_TOPK + r, 1), :] = pos
        return carry

    lax.fori_loop(0, PEER_TOPK, round1, 0)

    ones_row = (lax.broadcasted_iota(jnp.int32, (NSEL_EXT - 2 * PEER_TOPK, tm), 0) == 0).astype(F32)
    sv_scr[2 * PEER_TOPK:, :] = ones_row
    si_scr[2 * PEER_TOPK:, :] = ones_row
    c_scr[...] = sum(jnp.dot(expand_ref[...], p, preferred_element_type=F32) for p in _split3(sv_scr[...]))
    enc_scr[...] = jnp.dot(scale_ref[...], si_scr[...].astype(BF16), preferred_element_type=F32)
    big = float(NCAND * ENC_SHIFT)

    def round2(r, carry):
        c = c_scr[...]
        enc = enc_scr[...]
        m = jnp.max(c, axis=0, keepdims=True)
        e = jnp.min(jnp.where(c == m, enc, big), axis=0, keepdims=True)
        c_scr[...] = jnp.where(enc == e, NEG_INF, c)
        top_scr[pl.ds(r, 1), :] = m
        e_scr[pl.ds(r, 1), :] = e
        return carry

    lax.fori_loop(0, PEER_TOPK, round2, 0)
    top = top_scr[...]
    ex = jnp.exp(top - jnp.max(top, axis=0, keepdims=True))
    gate_ref[0] = ex / jnp.sum(ex, axis=0, keepdims=True)
    eidx_ref[0] = e_scr[...].astype(jnp.int32) & (ENC_SHIFT - 1)


def peer_topk(q, subkeys, *, tm=512):
    T = q.shape[0]
    assert T % tm == 0
    vm = pltpu.VMEM
    eidx, gate = pl.pallas_call(
        _peer_topk_kernel,
        grid=(T // tm, PEER_HEADS),
        in_specs=[pl.BlockSpec((tm, PEER_DQ), lambda i, h: (i, h)),
                  pl.BlockSpec((1, 2, PEER_NKEYS, PEER_DQ // 2), lambda i, h: (h, 0, 0, 0)),
                  pl.BlockSpec((NCAND_KEPT, NSEL_EXT), lambda i, h: (0, 0)),
                  pl.BlockSpec((NCAND_KEPT, NSEL_EXT), lambda i, h: (0, 0))],
        out_specs=[pl.BlockSpec((1, PEER_TOPK, tm), lambda i, h: (h, 0, i)),
                   pl.BlockSpec((1, PEER_TOPK, tm), lambda i, h: (h, 0, i))],
        out_shape=[jax.ShapeDtypeStruct((PEER_HEADS, PEER_TOPK, T), jnp.int32),
                   jax.ShapeDtypeStruct((PEER_HEADS, PEER_TOPK, T), F32)],
        scratch_shapes=[vm((2, PEER_NKEYS, tm), F32), vm((NSEL_EXT, tm), F32), vm((NSEL_EXT, tm), F32),
                        vm((NCAND_KEPT, tm), F32), vm((NCAND_KEPT, tm), F32), vm((PEER_TOPK, tm), F32),
                        vm((PEER_TOPK, tm), F32)],
        compiler_params=pltpu.CompilerParams(dimension_semantics=("parallel", "arbitrary")),
        name="peer_topk",
    )(q, subkeys.astype(BF16), *candidate_matrices())
    fix = lambda a: a.transpose(2, 0, 1).reshape(T, PEER_HEADS * PEER_TOPK)
    return fix(eidx), fix(gate)


def _fold(xa, xb, mask, shift):
    s = jnp.where(mask, xa, xb)
    t = jnp.where(mask, xb, xa)
    return s + pltpu.roll(t, shift, axis=0)


def _unpack(w):
    lo = lax.bitcast_convert_type(w << 16, F32)
    hi = lax.bitcast_convert_type(w & jnp.int32(-65536), F32)
    return lo, hi


def _peer_dot_kernel(meta_ref, *refs, chunk):
    row_refs = refs[:SUBLANES]
    gtok_ref, x_ref, gate_ref, u_ref, a_ref, c_scr = refs[SUBLANES:]
    k = pl.program_id(1)
    gpc = chunk // SUBLANES
    c0 = meta_ref[0, 0, k]
    c1 = meta_ref[0, 0, k + 1]

    @pl.when(k == 0)
    def _():
        a_ref[...] = jnp.zeros_like(a_ref)

    c_scr[...] = jnp.zeros_like(c_scr)
    row_i = lax.broadcasted_iota(jnp.int32, (SUBLANES, LANES), 0)
    m1 = (row_i & 1) == 0
    m2 = (row_i & 2) == 0
    m3 = (row_i & 4) == 0
    ones = jnp.ones((SUBLANES, LANES), BF16)
    contract_last = (((1,), (1,)), ((), ()))

    def finish(c, buf):
        h8 = sum(lax.dot_general(ones, p, contract_last, preferred_element_type=F32) for p in _split2(c_scr[buf]))
        h = h8[0:1, :]
        off = pl.multiple_of(c * chunk, chunk)
        act = 0.5 * h * (1.0 + lax.erf(h * 0.7071067811865476))
        a_ref[0, :, pl.ds(off, chunk)] = act * gate_ref[0, :, pl.ds(off, chunk)]

    def chunk_body(c, carry):
        buf = (c - c0) & 1
        finish(jnp.maximum(c - 1, c0), 1 - buf)
        for gl in range(gpc):
            g = c * gpc + gl
            tok = gtok_ref[0, 0, g]
            xlo = x_ref[tok, 0:P_ROWS, :]
            xhi = x_ref[tok, P_ROWS:D_ROWS, :]
            ps = []
            for i in range(SUBLANES):
                r8 = pl.multiple_of(row_refs[i][0, 0, g], SUBLANES)
                lo, hi = _unpack(u_ref[pl.ds(r8, SUBLANES), :])
                ps.append(lo * xlo + hi * xhi)
            q = [_fold(ps[2 * i], ps[2 * i + 1], m1, 1) for i in range(4)]
            r = [_fold(q[0], q[1], m2, 2), _fold(q[2], q[3], m2, 2)]
            c_scr[buf, gl * SUBLANES:(gl + 1) * SUBLANES, :] = _fold(r[0], r[1], m3, 4)
        return carry

    lax.fori_loop(c0, c1, chunk_body, 0)

    @pl.when(c1 > c0)
    def _():
        finish(c1 - 1, (c1 - 1 - c0) & 1)


def _peer_acc_kernel(meta_ref, *refs, chunk):
    row_refs = refs[:SUBLANES]
    a_refs = refs[SUBLANES:2 * SUBLANES]
    gtok_ref, res_ref, v_ref, o_ref = refs[2 * SUBLANES:]
    k = pl.program_id(1)
    gpc = chunk // SUBLANES

    @pl.when(k == 0)
    def _():
        o_ref[...] = res_ref[...]

    def groups(gp, carry):
        for sub in range(GROUP_UNROLL):
            g = gp * GROUP_UNROLL + sub
            tok = gtok_ref[0, 0, g]
            alo = ahi = None
            for i in range(SUBLANES):
                r8 = pl.multiple_of(row_refs[i][0, 0, g], SUBLANES)
                lo, hi = _unpack(v_ref[pl.ds(r8, SUBLANES), :])
                a = a_refs[i][0, 0, g]
                alo = a * lo if alo is None else alo + a * lo
                ahi = a * hi if ahi is None else ahi + a * hi
            o_ref[tok, 0:P_ROWS, :] += alo
            o_ref[tok, P_ROWS:D_ROWS, :] += ahi
        return carry

    per_chunk = gpc // GROUP_UNROLL
    lax.fori_loop(meta_ref[0, 0, k] * per_chunk, meta_ref[0, 0, k + 1] * per_chunk, groups, 0)


def pack_table(w):
    e = w.shape[0]
    bits = lax.bitcast_convert_type(w.astype(BF16), jnp.uint16).astype(jnp.uint32)
    half = D_MODEL // 2
    packed = (bits[:, half:] << 16) | bits[:, :half]
    return lax.bitcast_convert_type(packed, jnp.int32).reshape(e * P_ROWS, LANES)


def by_lane(a):
    nb, cap = a.shape[0], a.shape[-1]
    a = a.reshape(nb, cap // SUBLANES, SUBLANES)
    return [a[:, :, i].reshape(nb, 1, cap // SUBLANES) for i in range(SUBLANES)]


def peer_route(eidx, gate, *, n_exp, tile, blk, chunk):
    T, S = eidx.shape
    nb = T // blk
    nt = n_exp // tile
    shift = tile.bit_length() - 1
    assert 1 << shift == tile and T % blk == 0 and S + SUBLANES <= 256 and chunk <= 256
    assert blk & (blk - 1) == 0 and blk < 256 and shift <= ROUTE_POS_SHIFT and nt <= 4
    i32 = jnp.int32
    tiles = jnp.arange(nt, dtype=i32)
    tile_id = eidx >> shift
    cnt = jnp.sum(tile_id[:, :, None] == tiles[None, None, :], axis=1, dtype=i32)
    tpad = (-cnt) % SUBLANES
    t_in = (jnp.arange(T, dtype=i32) % blk)[:, None]
    dead = i32(nt << ROUTE_TILE_SHIFT)
    tile_f, tok_f, pos_f = ROUTE_TILE_SHIFT, ROUTE_TOK_SHIFT, ROUTE_POS_SHIFT
    key_real = ((tile_id << tile_f) | (t_in << tok_f) | (jnp.arange(S, dtype=i32)[None, :] << pos_f)
                | (eidx & (tile - 1)))
    i8 = jnp.arange(SUBLANES, dtype=i32)
    key_tpad = jnp.where(i8[None, None, :] < tpad[:, :, None],
                         (tiles[None, :, None] << tile_f) | (t_in[:, :, None] << tok_f)
                         | ((S + i8)[None, None, :] << pos_f), dead)
    n_run = jnp.sum((cnt + tpad).reshape(nb, blk, nt), axis=1)
    bpad = (-n_run) % chunk
    ib = jnp.arange(chunk - SUBLANES, dtype=i32)
    key_bpad = jnp.where(ib[None, None, :] < bpad[:, :, None],
                         (tiles[None, :, None] << tile_f) | i32(blk << tok_f) | (ib[None, None, :] << pos_f), dead)
    n_real, n_tp, n_bp = blk * S, blk * nt * SUBLANES, nt * (chunk - SUBLANES)
    cap = -(-(n_real + n_tp + nt * chunk) // 1024) * 1024
    fill = cap - (n_real + n_tp + n_bp)
    keys = jnp.concatenate([key_real.reshape(nb, n_real), key_tpad.reshape(nb, n_tp), key_bpad.reshape(nb, n_bp),
                            jnp.full((nb, fill), dead, i32)], axis=1)
    gates = jnp.concatenate([gate.reshape(nb, n_real), jnp.zeros((nb, cap - n_real), gate.dtype)], axis=1)
    keys, gates = lax.sort((keys, gates), dimension=1, num_keys=1)
    rows = (keys & ((1 << pos_f) - 1)) * SUBLANES
    gtok = (keys[:, ::SUBLANES] >> tok_f) & (blk - 1)
    n_chunks = (n_run + bpad) // chunk
    cstart = jnp.concatenate([jnp.zeros((nb, 1), i32), jnp.cumsum(n_chunks, axis=1)], axis=1)
    meta = jnp.pad(cstart, ((0, 0), (0, LANES - nt - 1)))
    r3 = lambda a: a.reshape(nb, 1, a.shape[1])
    return r3(meta), by_lane(rows), r3(gtok), r3(gates), cap


def peer_experts(h, res, eidx, gate, u_packed, v_packed, *, tile, blk):
    T = h.shape[0]
    chunk = PEER_CHUNK_SLOTS
    n_exp = u_packed.shape[0] // P_ROWS
    nt = n_exp // tile
    nb = T // blk
    meta, rows, gtok, gates, cap = peer_route(eidx, gate, n_exp=n_exp, tile=tile, blk=blk, chunk=chunk)
    tok_spec = pl.BlockSpec((blk, D_ROWS, LANES), lambda j, k: (j, 0, 0))
    tab_spec = pl.BlockSpec((tile * P_ROWS, LANES), lambda j, k: (k, 0))
    smem = lambda n: pl.BlockSpec((1, 1, n), lambda j, k: (j, 0, 0), memory_space=pltpu.SMEM,
                                  pipeline_mode=pl.Buffered(1))
    slot_spec = pl.BlockSpec((1, 1, cap), lambda j, k: (j, 0, 0))
    params = pltpu.CompilerParams(dimension_semantics=("parallel", "arbitrary"), vmem_limit_bytes=PEER_VMEM_LIMIT)
    a = pl.pallas_call(
        functools.partial(_peer_dot_kernel, chunk=chunk),
        grid=(nb, nt),
        in_specs=[smem(LANES)] + [smem(cap // SUBLANES)] * (SUBLANES + 1) + [tok_spec, slot_spec, tab_spec],
        out_specs=slot_spec,
        out_shape=jax.ShapeDtypeStruct((nb, 1, cap), F32),
        scratch_shapes=[pltpu.VMEM((2, chunk, LANES), F32)],
        compiler_params=params, name="peer_dot",
    )(meta, *rows, gtok, h.reshape(T, D_ROWS, LANES), gates, u_packed)
    out = pl.pallas_call(
        functools.partial(_peer_acc_kernel, chunk=chunk),
        grid=(nb, nt),
        in_specs=[smem(LANES)] + [smem(cap // SUBLANES)] * (2 * SUBLANES + 1) + [tok_spec, tab_spec],
        out_specs=tok_spec,
        out_shape=jax.ShapeDtypeStruct((T, D_ROWS, LANES), F32),
        compiler_params=params, name="peer_acc",
    )(meta, *rows, *by_lane(a.reshape(nb, cap)), gtok, res.reshape(T, D_ROWS, LANES), v_packed)
    return out.reshape(T, D_MODEL)


ATT_W = 3 * N_DIL * D_ATT
ROPE_HALF = ROPE_DIM // 2
ATT_TQ = 128
N_QK_BLOCKS = 2 * N_DIL


def rope_lane_tables(length):
    inv = ROPE_THETA ** (-jnp.arange(0, ROPE_DIM, 2, dtype=F32) / ROPE_DIM)
    ang = jnp.arange(length, dtype=F32)[:, None] * inv[None, :]
    cos, sin = jnp.cos(ang), jnp.sin(ang)
    lane = np.arange(LANES) % HEAD_DIM
    j = lane % ROPE_HALF
    first, second = lane < ROPE_HALF, (lane >= ROPE_HALF) & (lane < ROPE_DIM)
    c = jnp.where((first | second)[None, :], cos[:, j], 1.0)
    s1 = jnp.where(second[None, :], sin[:, j], 0.0)
    s2 = jnp.where(first[None, :], -sin[:, j], 0.0)
    return c, s1, s2


def _rope_cast_kernel(x_ref, c_ref, s1_ref, s2_ref, o_ref, *, q_scale):
    col = pl.program_id(1)

    @pl.when(col < N_QK_BLOCKS)
    def _():
        scale = jnp.where(col < N_DIL, q_scale, 1.0)
        c, s1, s2 = c_ref[...], s1_ref[...], s2_ref[...]
        for t in range(D_ATT // LANES):
            x = x_ref[:, t * LANES:(t + 1) * LANES]
            y = x * c + pltpu.roll(x, ROPE_HALF, axis=1) * s1 + pltpu.roll(x, LANES - ROPE_HALF, axis=1) * s2
            o_ref[:, t * LANES:(t + 1) * LANES] = (y * scale).astype(BF16)

    @pl.when(col >= N_QK_BLOCKS)
    def _():
        o_ref[...] = x_ref[...].astype(BF16)


def rope_cast(att, seq_len, *, tm=512):
    T = att.shape[0]
    assert seq_len % tm == 0
    c, s1, s2 = rope_lane_tables(seq_len)
    nblk = seq_len // tm
    tab = pl.BlockSpec((tm, LANES), lambda i, j: (i % nblk, 0))
    blk = pl.BlockSpec((tm, D_ATT), lambda i, j: (i, j))
    return pl.pallas_call(
        functools.partial(_rope_cast_kernel, q_scale=HEAD_DIM ** -0.5),
        grid=(T // tm, ATT_W // D_ATT),
        in_specs=[blk, tab, tab, tab],
        out_specs=blk,
        out_shape=jax.ShapeDtypeStruct((T, ATT_W), BF16),
        compiler_params=pltpu.CompilerParams(dimension_semantics=("parallel", "parallel")),
        name="rope_cast",
    )(att, c, s1, s2)


def _band_attn_kernel(q_ref, kp_ref, kc_ref, kn_ref, vp_ref, vc_ref, vn_ref, o_ref, lse_ref, s_scr, p_scr,
                      *, hw, sub_len):
    i = pl.program_id(2)
    tq = q_ref.shape[0]
    n_heads = D_ATT // HEAD_DIM
    q_pos = i * tq + lax.broadcasted_iota(jnp.int32, (tq, 3 * tq), 0)
    k_pos = (i - 1) * tq + lax.broadcasted_iota(jnp.int32, (tq, 3 * tq), 1)
    valid = (jnp.abs(k_pos - q_pos) <= hw) & (k_pos >= 0) & (k_pos < sub_len)
    low = lax.broadcasted_iota(jnp.int32, (tq, LANES), 1) < HEAD_DIM
    contract_last = (((1,), (1,)), ((), ()))
    k_refs = (kp_ref, kc_ref, kn_ref)
    v_refs = (vp_ref, vc_ref, vn_ref)
    for h in range(n_heads):
        sl = slice(h // 2 * LANES, (h // 2 + 1) * LANES)
        q2 = q_ref[:, sl]
        qh = jnp.where(low if h % 2 == 0 else ~low, q2, jnp.zeros_like(q2))
        for j in range(3):
            s_scr[h, :, j * tq:(j + 1) * tq] = lax.dot_general(qh, k_refs[j][:, sl], contract_last,
                                                                 preferred_element_type=F32)
    s = jnp.where(valid[None], s_scr[...], NEG)
    m = jnp.max(s, axis=-1, keepdims=True)
    p = jnp.exp(s - m)
    l = jnp.sum(p, axis=-1, keepdims=True)
    p_scr[...] = p.astype(BF16)
    lse = m + jnp.log(l)
    inv = 1.0 / l
    for pair in range(n_heads // 2):
        sl = slice(pair * LANES, (pair + 1) * LANES)
        os = []
        for h in (2 * pair, 2 * pair + 1):
            o = sum(jnp.dot(p_scr[h, :, j * tq:(j + 1) * tq], v_refs[j][:, sl], preferred_element_type=F32)
                    for j in range(3))
            os.append(o * inv[h])
        o_ref[:, sl] = jnp.where(low, os[0], os[1])
        lse_ref[:, sl] = jnp.where(low, lse[2 * pair], lse[2 * pair + 1])


def band_attention(qkv, g, batch, seq_len, *, tq=ATT_TQ):
    win, dil = DILATION_CFG[g]
    hw = win // (2 * dil)
    sub_len = seq_len // dil
    assert sub_len % tq == 0 and hw <= tq
    nq = sub_len // tq
    ncol = ATT_W // D_ATT
    x = qkv.reshape(batch, sub_len, dil * ATT_W)

    def spec(kind, shift):
        def index(b, r, i):
            return (b, jnp.clip(i + shift, 0, nq - 1), r * ncol + kind * N_DIL + g)
        return pl.BlockSpec((None, tq, D_ATT), index)

    out_spec = pl.BlockSpec((None, tq, D_ATT), lambda b, r, i: (b, i, r))
    o, lse = pl.pallas_call(
        functools.partial(_band_attn_kernel, hw=hw, sub_len=sub_len),
        grid=(batch, dil, nq),
        in_specs=[spec(0, 0), spec(1, -1), spec(1, 0), spec(1, 1), spec(2, -1), spec(2, 0), spec(2, 1)],
        out_specs=[out_spec, out_spec],
        out_shape=[jax.ShapeDtypeStruct((batch, sub_len, dil * D_ATT), F32)] * 2,
        scratch_shapes=[pltpu.VMEM((D_ATT // HEAD_DIM, tq, 3 * tq), F32),
                        pltpu.VMEM((D_ATT // HEAD_DIM, tq, 3 * tq), BF16)],
        compiler_params=pltpu.CompilerParams(dimension_semantics=("parallel", "parallel", "parallel"),
                                             vmem_limit_bytes=VMEM_LIMIT_BYTES),
        name=f"band_attn_d{dil}",
    )(x, x, x, x, x, x, x)
    return o.reshape(batch * seq_len, D_ATT), lse.reshape(batch * seq_len, D_ATT)


def _attn_merge_kernel(o0, o1, o2, l0, l1, l2, g_ref, y_ref):
    ls = [l0[...], l1[...], l2[...]]
    m = jnp.maximum(jnp.maximum(ls[0], ls[1]), ls[2])
    es = [jnp.exp(l - m) for l in ls]
    den = es[0] + es[1] + es[2]
    o = (es[0] * o0[...] + es[1] * o1[...] + es[2] * o2[...]) / den
    ms = jnp.mean(o * o, axis=-1, keepdims=True)
    y_ref[...] = o * lax.rsqrt(ms + EPS) * g_ref[...]


def attn_merge(outs, lses, norm_w, *, tm=512):
    T = outs[0].shape[0]
    blk = pl.BlockSpec((tm, D_ATT), lambda i: (i, 0))
    return pl.pallas_call(
        _attn_merge_kernel,
        grid=(T // tm,),
        in_specs=[blk] * 6 + [pl.BlockSpec((1, D_ATT), lambda i: (0, 0))],
        out_specs=blk,
        out_shape=jax.ShapeDtypeStruct((T, D_ATT), F32),
        compiler_params=pltpu.CompilerParams(dimension_semantics=("parallel",)),
        name="attn_merge",
    )(*outs, *lses, norm_w.reshape(1, D_ATT))


def attention_mixer(att, norm_w, batch, seq_len):
    qkv = rope_cast(att, seq_len)
    res = [band_attention(qkv, g, batch, seq_len) for g in range(N_DIL)]
    return attn_merge([r[0] for r in res], [r[1] for r in res], norm_w)


CONV_HALO = SUBLANES


def _conv_taps(x, w_ref, pad_scr, width):
    n = x.shape[0]
    half = width // 2
    zeros = jnp.zeros((CONV_HALO, LANES), F32)
    pad_scr[0:CONV_HALO, :] = zeros
    pad_scr[CONV_HALO + n:CONV_HALO + n + CONV_HALO, :] = zeros
    pad_scr[CONV_HALO:CONV_HALO + n, :] = x
    acc = jnp.zeros((n, LANES), F32)
    for k in range(width):
        acc = acc + pad_scr[CONV_HALO - half + k:CONV_HALO - half + k + n, :] * w_ref[k:k + 1, :]
    return acc


def _silu(x):
    return x * (1.0 / (1.0 + jnp.exp(-x)))


def _ssd_conv_kernel(x_ref, w_ref, b_ref, o_ref, pad_scr):
    o_ref[...] = _silu(_conv_taps(x_ref[...], w_ref, pad_scr, SSD_CONV) + b_ref[...])


def _shortconv_kernel(bg_ref, cg_ref, hx_ref, w_ref, o_ref, pad_scr):
    o_ref[...] = bg_ref[...] * _conv_taps(cg_ref[...] * hx_ref[...], w_ref, pad_scr, SC_CONV)


def _strip_call(kernel, name, n_out, batch, seq_len, in_specs, args):
    return pl.pallas_call(
        kernel,
        grid=(batch, n_out // LANES),
        in_specs=in_specs,
        out_specs=pl.BlockSpec((seq_len, LANES), lambda b, c: (b, c)),
        out_shape=jax.ShapeDtypeStruct((batch * seq_len, n_out), F32),
        scratch_shapes=[pltpu.VMEM((seq_len + 2 * CONV_HALO, LANES), F32)],
        compiler_params=pltpu.CompilerParams(dimension_semantics=("parallel", "parallel"),
                                             vmem_limit_bytes=VMEM_LIMIT_BYTES),
        name=name,
    )(*args)


def ssd_conv(xbc, w, bias, batch, seq_len):
    strip = lambda off: pl.BlockSpec((seq_len, LANES), lambda b, c: (b, c + off))
    taps = pl.BlockSpec((SSD_CONV, LANES), lambda b, c: (0, c))
    vec = pl.BlockSpec((1, LANES), lambda b, c: (0, c))
    return _strip_call(_ssd_conv_kernel, "ssd_conv", XBC, batch, seq_len, [strip(0), taps, vec],
                       (xbc, w, bias.reshape(1, XBC)))


def shortconv(sc, w, batch, seq_len):
    nblk = D_SC // LANES
    strip = lambda off: pl.BlockSpec((seq_len, LANES), lambda b, c: (b, c + off))
    taps = pl.BlockSpec((SC_CONV, LANES), lambda b, c: (0, c))
    return _strip_call(_shortconv_kernel, "shortconv", D_SC, batch, seq_len,
                       [strip(0), strip(nblk), strip(2 * nblk), taps], (sc, sc, sc, w))


HEAD_PAIRS = D_SSD // LANES
PAIRS_PER_GROUP = HEAD_PAIRS // SSD_GROUPS


def _dot3(x, w):
    return sum(jnp.dot(p, w, preferred_element_type=F32) for p in _split3(x))


def _softplus(x):
    return jnp.maximum(x, 0.0) + jnp.log(1.0 + jnp.exp(-jnp.abs(x)))


def _ssd_scan_kernel(xbc_ref, dt_ref, dtt_ref, bias_ref, a_ref, biast_ref, at_ref, erep_ref, eexp_ref,
                     y_ref, h_scr, cumrep_scr):
    d = pl.program_id(1)
    c = pl.program_id(2)
    q = SSD_CHUNK

    @pl.when(c == 0)
    def _():
        h_scr[...] = jnp.zeros_like(h_scr)

    row = lax.broadcasted_iota(jnp.int32, (q, q), 0)
    col = lax.broadcasted_iota(jnp.int32, (q, q), 1)
    sign = jnp.where(d == 0, 1, -1)
    mask = (row - col) * sign >= 0
    mask01 = jnp.where(mask, 1.0, 0.0).astype(BF16)
    low = lax.broadcasted_iota(jnp.int32, (q, LANES), 1) < HEAD_DIM
    contract_last = (((1,), (1,)), ((), ()))
    contract_first = (((0,), (0,)), ((), ()))

    dt = _softplus(dt_ref[...] + bias_ref[...])
    a = dt * a_ref[...]
    cum = sum(jnp.dot(mask01, p, preferred_element_type=F32) for p in _split3(a))
    eexp = eexp_ref[0]
    cumrep_scr[...] = _dot3(cum, erep_ref[0])
    cum_exp = _dot3(cum, eexp)
    dt_exp = _dot3(dt, eexp)
    total_exp = jnp.sum(_dot3(a, eexp), axis=0, keepdims=True)

    heads = pl.ds(pl.multiple_of(d * SSD_HEADS, SUBLANES), SSD_HEADS)
    at = _softplus(dtt_ref[heads, :] + biast_ref[heads, :]) * at_ref[heads, :]
    cum_t = sum(lax.dot_general(p, mask01, contract_last, preferred_element_type=F32) for p in _split3(at))

    for g in range(SSD_GROUPS):
        b_g = xbc_ref[:, D_SSD + g * SSD_STATE:D_SSD + (g + 1) * SSD_STATE].astype(BF16)
        c_g = xbc_ref[:, D_SSD + (SSD_GROUPS + g) * SSD_STATE:D_SSD + (SSD_GROUPS + g + 1) * SSD_STATE].astype(BF16)
        cb = lax.dot_general(c_g, b_g, contract_last, preferred_element_type=F32)
        for pp in range(PAIRS_PER_GROUP):
            p = g * PAIRS_PER_GROUP + pp
            sl = slice(p * LANES, (p + 1) * LANES)
            xdt = xbc_ref[:, sl] * dt_exp[:, sl]
            xdt_b = xdt.astype(BF16)
            yd = []
            for hh in range(2):
                h = 2 * p + hh
                seg = cumrep_scr[:, h * LANES:(h + 1) * LANES] - cum_t[h:h + 1, :]
                m = (cb * jnp.exp(jnp.where(mask, seg, NEG_INF))).astype(BF16)
                yd.append(jnp.dot(m, xdt_b, preferred_element_type=F32))
            cum_p = cum_exp[:, sl]
            h_in = h_scr[p]
            y_off = jnp.dot(c_g, h_in.astype(BF16), preferred_element_type=F32) * jnp.exp(cum_p)
            y_ref[0, :, sl] = jnp.where(low, yd[0], yd[1]) + y_off
            tot_p = total_exp[:, sl]
            xs_end = (xdt * jnp.exp(tot_p - cum_p)).astype(BF16)
            states = lax.dot_general(b_g, xs_end, contract_first, preferred_element_type=F32)
            h_scr[p] = jnp.exp(tot_p) * h_in + states


def head_expanders():
    erep = np.zeros((2, LANES, SSD_HEADS * LANES), np.float32)
    eexp = np.zeros((2, LANES, D_SSD), np.float32)
    for d in range(2):
        for h in range(SSD_HEADS):
            erep[d, d * SSD_HEADS + h, h * LANES:(h + 1) * LANES] = 1.0
            eexp[d, d * SSD_HEADS + h, h * HEAD_DIM:(h + 1) * HEAD_DIM] = 1.0
    return jnp.asarray(erep, BF16), jnp.asarray(eexp, BF16)


def ssd_scan(xbc_c, dt_raw, dt_bias, a_log, batch, seq_len):
    T = xbc_c.shape[0]
    assert seq_len % SSD_CHUNK == 0
    nc = seq_len // SSD_CHUNK
    lane_pad = LANES - 2 * SSD_HEADS
    dt_l = jnp.pad(dt_raw, ((0, 0), (0, lane_pad)))
    dt_t = dt_raw.T
    bias = dt_bias.reshape(2 * SSD_HEADS).astype(F32)
    a_neg = -jnp.exp(a_log.reshape(2 * SSD_HEADS).astype(F32))
    erep, eexp = head_expanders()

    def chunk(b, d, c):
        return b * nc + jnp.where(d == 0, c, nc - 1 - c)

    vec = pl.BlockSpec((1, LANES), lambda b, d, c: (0, 0))
    colv = pl.BlockSpec((2 * SSD_HEADS, 1), lambda b, d, c: (0, 0))
    return pl.pallas_call(
        _ssd_scan_kernel,
        grid=(batch, 2, nc),
        in_specs=[pl.BlockSpec((SSD_CHUNK, XBC), lambda b, d, c: (chunk(b, d, c), 0)),
                  pl.BlockSpec((SSD_CHUNK, LANES), lambda b, d, c: (chunk(b, d, c), 0)),
                  pl.BlockSpec((2 * SSD_HEADS, SSD_CHUNK), lambda b, d, c: (0, chunk(b, d, c))),
                  vec, vec, colv, colv,
                  pl.BlockSpec((1, LANES, SSD_HEADS * LANES), lambda b, d, c: (d, 0, 0)),
                  pl.BlockSpec((1, LANES, D_SSD), lambda b, d, c: (d, 0, 0))],
        out_specs=pl.BlockSpec((1, SSD_CHUNK, D_SSD), lambda b, d, c: (d, chunk(b, d, c), 0)),
        out_shape=jax.ShapeDtypeStruct((2, T, D_SSD), F32),
        scratch_shapes=[pltpu.VMEM((HEAD_PAIRS, SSD_STATE, LANES), F32),
                        pltpu.VMEM((SSD_CHUNK, SSD_HEADS * LANES), F32)],
        compiler_params=pltpu.CompilerParams(dimension_semantics=("parallel", "arbitrary", "arbitrary"),
                                             vmem_limit_bytes=VMEM_LIMIT_BYTES),
        name="ssd_scan",
    )(xbc_c, dt_l, dt_t, jnp.pad(bias, (0, lane_pad)).reshape(1, LANES),
      jnp.pad(a_neg, (0, lane_pad)).reshape(1, LANES), bias.reshape(2 * SSD_HEADS, 1),
      a_neg.reshape(2 * SSD_HEADS, 1), erep, eexp)


def _ssd_gate_norm_kernel(y_ref, xs_ref, z_ref, dskip_ref, nw_ref, o_ref):
    y = (y_ref[0] + y_ref[1] + xs_ref[...] * dskip_ref[...]) * _silu(z_ref[...])
    gw = D_SSD // SSD_GROUPS
    for g in range(SSD_GROUPS):
        yg = y[:, g * gw:(g + 1) * gw]
        ms = jnp.mean(yg * yg, axis=-1, keepdims=True)
        o_ref[:, g * gw:(g + 1) * gw] = yg * lax.rsqrt(ms + EPS) * nw_ref[:, g * gw:(g + 1) * gw]


def ssd_gate_norm(y2, xbc_c, z, d_skip, norm_w, *, tm=512):
    T = z.shape[0]
    dvec = jnp.repeat(d_skip.astype(F32), HEAD_DIM).reshape(1, D_SSD)
    blk = pl.BlockSpec((tm, D_SSD), lambda i: (i, 0))
    vec = pl.BlockSpec((1, D_SSD), lambda i: (0, 0))
    return pl.pallas_call(
        _ssd_gate_norm_kernel,
        grid=(T // tm,),
        in_specs=[pl.BlockSpec((2, tm, D_SSD), lambda i: (0, i, 0)), blk, blk, vec, vec],
        out_specs=blk,
        out_shape=jax.ShapeDtypeStruct((T, D_SSD), F32),
        compiler_params=pltpu.CompilerParams(dimension_semantics=("parallel",), vmem_limit_bytes=VMEM_LIMIT_BYTES),
        name="ssd_gate_norm",
    )(y2, xbc_c, z, dvec, norm_w.reshape(1, D_SSD))


def ssd_mixer(z, xbc, dt_raw, conv_w, conv_b, dt_bias, a_log, d_skip, norm_w, batch, seq_len):
    xbc_c = ssd_conv(xbc, conv_w, conv_b, batch, seq_len)
    y2 = ssd_scan(xbc_c, dt_raw, dt_bias, a_log, batch, seq_len)
    return ssd_gate_norm(y2, xbc_c, z, d_skip, norm_w)


def trunk(x, norm_mix, w_in, ssd_conv_w, ssd_conv_b, ssd_dt_bias, ssd_a_log, ssd_d, ssd_norm,
          sc_conv_w, sc_norm, att_norm, w_out, norm_ffn, peer_wq, peer_subkeys, peer_u, peer_v, norm_final):
    b, L, D = x.shape
    T = b * L
    x = x.reshape(T, D)
    o_xbc = D_SSD
    o_dt = o_xbc + XBC
    o_sc = o_dt + 2 * SSD_HEADS
    o_att = o_sc + 3 * D_SC
    for l in range(DEPTH):
        w = w_in[l]
        g = norm_mix[l]
        w_dt = jnp.pad(w[:, o_dt:o_sc], ((0, 0), (0, DT_PAD - 2 * SSD_HEADS)))
        z = norm_matmul(x, g, w[:, :o_xbc].astype(BF16), tn=768)
        xbc = norm_matmul(x, g, w[:, o_xbc:o_dt].astype(BF16), tn=640)
        dt_raw = norm_matmul(x, g, w_dt.astype(BF16))[:, :2 * SSD_HEADS]
        sc = norm_matmul(x, g, w[:, o_sc:o_att].astype(BF16), tn=768)
        att = norm_matmul(x, g, w[:, o_att:].astype(BF16), tn=768)
        ys = [ssd_mixer(z, xbc, dt_raw, ssd_conv_w[l], ssd_conv_b[l], ssd_dt_bias[l], ssd_a_log[l], ssd_d[l],
                        ssd_norm[l], b, L),
              rmsnorm_rows(shortconv(sc, sc_conv_w[l], b, L), sc_norm[l]),
              attention_mixer(att, att_norm[l], b, L)]
        x = matmul_residual(ys, w_out[l].astype(BF16), x)
        q, h = norm_matmul_emit(x, norm_ffn[l], peer_wq[l].astype(BF16))
        eidx, gate = peer_topk(q, peer_subkeys[l])
        x = peer_experts(h, x, eidx, gate, pack_table(peer_u[l]), pack_table(peer_v[l]),
                         tile=PEER_TILE, blk=PEER_BLK)
    return rmsnorm_rows(x, norm_final).reshape(b, L, D)


def kernel(x_prompt, x_sample, norm_mix, w_in, ssd_conv_w, ssd_conv_b, ssd_dt_bias, ssd_a_log, ssd_d, ssd_norm,
           sc_conv_w, sc_norm, att_norm, w_out, norm_ffn, peer_wq, peer_subkeys, peer_u, peer_v, norm_final):
    nb = x_prompt.shape[0]
    x = jnp.concatenate([x_prompt, x_sample], axis=0)
    y = trunk(x, norm_mix, w_in, ssd_conv_w, ssd_conv_b, ssd_dt_bias, ssd_a_log, ssd_d, ssd_norm,
              sc_conv_w, sc_norm, att_norm, w_out, norm_ffn, peer_wq, peer_subkeys, peer_u, peer_v, norm_final)
    return (y[:nb], y[nb:])
```

```python
import functools

import jax
import jax.numpy as jnp
import numpy as np
from jax import lax
from jax.experimental import pallas as pl
from jax.experimental.pallas import tpu as pltpu

D_MODEL = 2048
DEPTH = 2
D_MIX = 3 * D_MODEL // 2
HEAD_DIM = 64
D_SSD = D_MIX // 2
SSD_HEADS = D_SSD // HEAD_DIM
SSD_GROUPS = 4
SSD_STATE = 128
SSD_CONV = 5
SSD_CHUNK = 128
XBC = D_SSD + 2 * SSD_GROUPS * SSD_STATE
D_SC = D_MIX // 4
SC_CONV = 3
D_ATT = D_MIX // 4
DILATION_CFG = ((128, 1), (512, 4), (2048, 16))
N_DIL = len(DILATION_CFG)
ROPE_DIM = HEAD_DIM // 4
ROPE_THETA = 500000.0
PEER_HEADS = 8
PEER_NKEYS = 128
PEER_TOPK = 16
PEER_DQ = 256
EPS = 1e-6
NEG = -1e30
DT_PAD = 128

LANES = 128
SUBLANES = 8
D_ROWS = D_MODEL // LANES
P_ROWS = D_ROWS // 2
PEER_TILE = 4096
PEER_BLK = 128
PEER_CHUNK_SLOTS = 256
GROUP_UNROLL = 16
ROUTE_POS_SHIFT = 12
ROUTE_TOK_SHIFT = 20
ROUTE_TILE_SHIFT = 28

VMEM_LIMIT_BYTES = 56 * 1024 * 1024
PEER_VMEM_LIMIT = VMEM_LIMIT_BYTES

F32 = jnp.float32
BF16 = jnp.bfloat16


def _norm_matmul_kernel(x_ref, g_ref, w_ref, o_ref, h_scr):
    @pl.when(pl.program_id(1) == 0)
    def _():
        x = x_ref[...]
        ms = jnp.mean(x * x, axis=-1, keepdims=True)
        h_scr[...] = (x * lax.rsqrt(ms + EPS) * g_ref[...]).astype(BF16)

    o_ref[...] = jnp.dot(h_scr[...], w_ref[...], preferred_element_type=F32)


def norm_matmul(x, g, w, *, tm=1024, tn=512):
    m, k = x.shape
    n = w.shape[1]
    tn = min(tn, n)
    assert m % tm == 0 and n % tn == 0
    return pl.pallas_call(
        _norm_matmul_kernel,
        grid=(m // tm, n // tn),
        in_specs=[
            pl.BlockSpec((tm, k), lambda i, j: (i, 0)),
            pl.BlockSpec((1, k), lambda i, j: (0, 0)),
            pl.BlockSpec((k, tn), lambda i, j: (0, j)),
        ],
        out_specs=pl.BlockSpec((tm, tn), lambda i, j: (i, j)),
        out_shape=jax.ShapeDtypeStruct((m, n), F32),
        scratch_shapes=[pltpu.VMEM((tm, k), BF16)],
        compiler_params=pltpu.CompilerParams(
            dimension_semantics=("parallel", "arbitrary"),
            vmem_limit_bytes=VMEM_LIMIT_BYTES),
        name="norm_matmul",
    )(x, g.reshape(1, k), w)


def _norm_matmul_emit_kernel(x_ref, g_ref, w_ref, o_ref, h_ref, h_scr):
    @pl.when(pl.program_id(1) == 0)
    def _():
        x = x_ref[...]
        ms = jnp.mean(x * x, axis=-1, keepdims=True)
        h = x * lax.rsqrt(ms + EPS) * g_ref[...]
        h_ref[...] = h
        h_scr[...] = h.astype(BF16)

    o_ref[...] = jnp.dot(h_scr[...], w_ref[...], preferred_element_type=F32)


def norm_matmul_emit(x, g, w, *, tm=512, tn=1024):
    m, k = x.shape
    n = w.shape[1]
    assert m % tm == 0 and n % tn == 0
    return pl.pallas_call(
        _norm_matmul_emit_kernel,
        grid=(m // tm, n // tn),
        in_specs=[
            pl.BlockSpec((tm, k), lambda i, j: (i, 0)),
            pl.BlockSpec((1, k), lambda i, j: (0, 0)),
            pl.BlockSpec((k, tn), lambda i, j: (0, j)),
        ],
        out_specs=[pl.BlockSpec((tm, tn), lambda i, j: (i, j)), pl.BlockSpec((tm, k), lambda i, j: (i, 0))],
        out_shape=[jax.ShapeDtypeStruct((m, n), F32), jax.ShapeDtypeStruct((m, k), F32)],
        scratch_shapes=[pltpu.VMEM((tm, k), BF16)],
        compiler_params=pltpu.CompilerParams(
            dimension_semantics=("parallel", "arbitrary"),
            vmem_limit_bytes=VMEM_LIMIT_BYTES),
        name="norm_matmul_emit",
    )(x, g.reshape(1, k), w)


def _matmul_residual_kernel(*refs, widths):
    y_refs, (w_ref, r_ref, o_ref) = refs[:len(widths)], refs[len(widths):]
    acc = r_ref[...]
    row = 0
    for y_ref, width in zip(y_refs, widths):
        acc = acc + jnp.dot(y_ref[...].astype(BF16), w_ref[row:row + width, :], preferred_element_type=F32)
        row += width
    o_ref[...] = acc


def matmul_residual(ys, w, res, *, tm=1024, tn=512):
    m = res.shape[0]
    widths = tuple(y.shape[1] for y in ys)
    k, n = w.shape
    assert m % tm == 0 and n % tn == 0 and sum(widths) == k
    return pl.pallas_call(
        functools.partial(_matmul_residual_kernel, widths=widths),
        grid=(m // tm, n // tn),
        in_specs=[pl.BlockSpec((tm, width), lambda i, j: (i, 0)) for width in widths] + [
            pl.BlockSpec((k, tn), lambda i, j: (0, j)),
            pl.BlockSpec((tm, tn), lambda i, j: (i, j)),
        ],
        out_specs=pl.BlockSpec((tm, tn), lambda i, j: (i, j)),
        out_shape=jax.ShapeDtypeStruct((m, n), F32),
        compiler_params=pltpu.CompilerParams(
            dimension_semantics=("parallel", "arbitrary"),
            vmem_limit_bytes=VMEM_LIMIT_BYTES),
        name="matmul_residual",
    )(*ys, w, res)


def _rmsnorm_kernel(x_ref, g_ref, o_ref):
    x = x_ref[...]
    ms = jnp.mean(x * x, axis=-1, keepdims=True)
    o_ref[...] = x * lax.rsqrt(ms + EPS) * g_ref[...]


def rmsnorm_rows(x, g, *, tm=512):
    m, k = x.shape
    return pl.pallas_call(
        _rmsnorm_kernel,
        grid=(m // tm,),
        in_specs=[pl.BlockSpec((tm, k), lambda i: (i, 0)),
                  pl.BlockSpec((1, k), lambda i: (0, 0))],
        out_specs=pl.BlockSpec((tm, k), lambda i: (i, 0)),
        out_shape=jax.ShapeDtypeStruct((m, k), F32),
        compiler_params=pltpu.CompilerParams(dimension_semantics=("parallel",)),
        name="rmsnorm",
    )(x, g.reshape(1, k))


NCAND = PEER_TOPK * PEER_TOPK
ENC_SHIFT = PEER_NKEYS * PEER_NKEYS
CAND_AB = [(a, b) for a in range(PEER_TOPK) for b in range(PEER_TOPK) if (a + 1) * (b + 1) <= PEER_TOPK]
NCAND_KEPT = -(-len(CAND_AB) // SUBLANES) * SUBLANES
NSEL_EXT = 2 * PEER_TOPK + SUBLANES
NEG_INF = float("-inf")


def candidate_matrices():
    expand = np.zeros((NCAND_KEPT, NSEL_EXT), np.float32)
    scale = np.zeros((NCAND_KEPT, NSEL_EXT), np.float32)
    for j, (a, b) in enumerate(CAND_AB):
        expand[j, a] = expand[j, PEER_TOPK + b] = 1.0
        scale[j, a], scale[j, PEER_TOPK + b] = PEER_NKEYS, 1.0
        scale[j, 2 * PEER_TOPK] = (a * PEER_TOPK + b) * ENC_SHIFT
    expand[len(CAND_AB):, 2 * PEER_TOPK] = -1e30
    scale[len(CAND_AB):, 2 * PEER_TOPK] = (NCAND - 1) * ENC_SHIFT
    return jnp.asarray(expand, BF16), jnp.asarray(scale, BF16)


def _split2(x):
    p0 = x.astype(BF16)
    return p0, (x - p0.astype(F32)).astype(BF16)


def _split3(x):
    p0 = x.astype(BF16)
    r = x - p0.astype(F32)
    p1 = r.astype(BF16)
    p2 = (r - p1.astype(F32)).astype(BF16)
    return p0, p1, p2


def _peer_topk_kernel(q_ref, k_ref, expand_ref, scale_ref, eidx_ref, gate_ref,
                      s_scr, sv_scr, si_scr, c_scr, enc_scr, top_scr, e_scr):
    tm = q_ref.shape[0]
    half = PEER_DQ // 2
    key_f = lax.broadcasted_iota(jnp.int32, (PEER_NKEYS, tm), 0).astype(F32)
    contract_last = (((1,), (1,)), ((), ()))

    for i in range(2):
        q = q_ref[:, i * half:(i + 1) * half].astype(BF16)
        s_scr[i] = lax.dot_general(k_ref[0, i], q, contract_last, preferred_element_type=F32)

    def round1(r, carry):
        for i in range(2):
            s = s_scr[i]
            m = jnp.max(s, axis=0, keepdims=True)
            pos = jnp.min(jnp.where(s == m, key_f, float(PEER_NKEYS)), axis=0, keepdims=True)
            s_scr[i] = jnp.where(key_f == pos, NEG_INF, s)
            sv_scr[pl.ds(i * PEER_TOPK + r, 1), :] = m
            si_scr[pl.ds(i * PEER_TOPK + r, 1), :] = pos
        return carry

    lax.fori_loop(0, PEER_TOPK, round1, 0)

    ones_row = (lax.broadcasted_iota(jnp.int32, (NSEL_EXT - 2 * PEER_TOPK, tm), 0) == 0).astype(F32)
    sv_scr[2 * PEER_TOPK:, :] = ones_row
    si_scr[2 * PEER_TOPK:, :] = ones_row
    c_scr[...] = sum(jnp.dot(expand_ref[...], p, preferred_element_type=F32) for p in _split3(sv_scr[...]))
    enc_scr[...] = jnp.dot(scale_ref[...], si_scr[...].astype(BF16), preferred_element_type=F32)
    big = float(NCAND * ENC_SHIFT)

    def round2(r, carry):
        c = c_scr[...]
        enc = enc_scr[...]
        m = jnp.max(c, axis=0, keepdims=True)
        e = jnp.min(jnp.where(c == m, enc, big), axis=0, keepdims=True)
        c_scr[...] = jnp.where(enc == e, NEG_INF, c)
        top_scr[pl.ds(r, 1), :] = m
        e_scr[pl.ds(r, 1), :] = e
        return carry

    lax.fori_loop(0, PEER_TOPK, round2, 0)
    top = top_scr[...]
    ex = jnp.exp(top - jnp.max(top, axis=0, keepdims=True))
    gate_ref[0] = ex / jnp.sum(ex, axis=0, keepdims=True)
    eidx_ref[0] = e_scr[...].astype(jnp.int32) & (ENC_SHIFT - 1)


def peer_topk(q, subkeys, *, tm=512):
    T = q.shape[0]
    assert T % tm == 0
    vm = pltpu.VMEM
    eidx, gate = pl.pallas_call(
        _peer_topk_kernel,
        grid=(T // tm, PEER_HEADS),
        in_specs=[pl.BlockSpec((tm, PEER_DQ), lambda i, h: (i, h)),
                  pl.BlockSpec((1, 2, PEER_NKEYS, PEER_DQ // 2), lambda i, h: (h, 0, 0, 0)),
                  pl.BlockSpec((NCAND_KEPT, NSEL_EXT), lambda i, h: (0, 0)),
                  pl.BlockSpec((NCAND_KEPT, NSEL_EXT), lambda i, h: (0, 0))],
        out_specs=[pl.BlockSpec((1, PEER_TOPK, tm), lambda i, h: (h, 0, i)),
                   pl.BlockSpec((1, PEER_TOPK, tm), lambda i, h: (h, 0, i))],
        out_shape=[jax.ShapeDtypeStruct((PEER_HEADS, PEER_TOPK, T), jnp.int32),
                   jax.ShapeDtypeStruct((PEER_HEADS, PEER_TOPK, T), F32)],
        scratch_shapes=[vm((2, PEER_NKEYS, tm), F32), vm((NSEL_EXT, tm), F32), vm((NSEL_EXT, tm), F32),
                        vm((NCAND_KEPT, tm), F32), vm((NCAND_KEPT, tm), F32), vm((PEER_TOPK, tm), F32),
                        vm((PEER_TOPK, tm), F32)],
        compiler_params=pltpu.CompilerParams(dimension_semantics=("parallel", "arbitrary")),
        name="peer_topk",
    )(q, subkeys.astype(BF16), *candidate_matrices())
    fix = lambda a: a.transpose(2, 0, 1).reshape(T, PEER_HEADS * PEER_TOPK)
    return fix(eidx), fix(gate)


def _fold(xa, xb, mask, shift):
    s = jnp.where(mask, xa, xb)
    t = jnp.where(mask, xb, xa)
    return s + pltpu.roll(t, shift, axis=0)


def _unpack(w):
    lo = lax.bitcast_convert_type(w << 16, F32)
    hi = lax.bitcast_convert_type(w & jnp.int32(-65536), F32)
    return lo, hi


def _peer_dot_kernel(meta_ref, *refs, chunk):
    row_refs = refs[:SUBLANES]
    gtok_ref, x_ref, gate_ref, u_ref, a_ref, c_scr = refs[SUBLANES:]
    k = pl.program_id(1)
    gpc = chunk // SUBLANES
    c0 = meta_ref[0, 0, k]
    c1 = meta_ref[0, 0, k + 1]

    @pl.when(k == 0)
    def _():
        a_ref[...] = jnp.zeros_like(a_ref)

    c_scr[...] = jnp.zeros_like(c_scr)
    row_i = lax.broadcasted_iota(jnp.int32, (SUBLANES, LANES), 0)
    m1 = (row_i & 1) == 0
    m2 = (row_i & 2) == 0
    m3 = (row_i & 4) == 0
    ones = jnp.ones((SUBLANES, LANES), BF16)
    contract_last = (((1,), (1,)), ((), ()))

    def finish(c, buf):
        h8 = sum(lax.dot_general(ones, p, contract_last, preferred_element_type=F32) for p in _split2(c_scr[buf]))
        h = h8[0:1, :]
        off = pl.multiple_of(c * chunk, chunk)
        act = 0.5 * h * (1.0 + lax.erf(h * 0.7071067811865476))
        a_ref[0, :, pl.ds(off, chunk)] = act * gate_ref[0, :, pl.ds(off, chunk)]

    def chunk_body(c, carry):
        buf = (c - c0) & 1
        finish(jnp.maximum(c - 1, c0), 1 - buf)
        for gl in range(gpc):
            g = c * gpc + gl
            tok = gtok_ref[0, 0, g]
            xlo = x_ref[tok, 0:P_ROWS, :]
            xhi = x_ref[tok, P_ROWS:D_ROWS, :]
            ps = []
            for i in range(SUBLANES):
                r8 = pl.multiple_of(row_refs[i][0, 0, g], SUBLANES)
                lo, hi = _unpack(u_ref[pl.ds(r8, SUBLANES), :])
                ps.append(lo * xlo + hi * xhi)
            q = [_fold(ps[2 * i], ps[2 * i + 1], m1, 1) for i in range(4)]
            r = [_fold(q[0], q[1], m2, 2), _fold(q[2], q[3], m2, 2)]
            c_scr[buf, gl * SUBLANES:(gl + 1) * SUBLANES, :] = _fold(r[0], r[1], m3, 4)
        return carry

    lax.fori_loop(c0, c1, chunk_body, 0)

    @pl.when(c1 > c0)
    def _():
        finish(c1 - 1, (c1 - 1 - c0) & 1)


def _peer_acc_kernel(meta_ref, *refs, chunk):
    row_refs = refs[:SUBLANES]
    a_refs = refs[SUBLANES:2 * SUBLANES]
    gtok_ref, res_ref, v_ref, o_ref = refs[2 * SUBLANES:]
    k = pl.program_id(1)
    gpc = chunk // SUBLANES

    @pl.when(k == 0)
    def _():
        o_ref[...] = res_ref[...]

    def groups(gp, carry):
        for sub in range(GROUP_UNROLL):
            g = gp * GROUP_UNROLL + sub
            tok = gtok_ref[0, 0, g]
            alo = ahi = None
            for i in range(SUBLANES):
                r8 = pl.multiple_of(row_refs[i][0, 0, g], SUBLANES)
                lo, hi = _unpack(v_ref[pl.ds(r8, SUBLANES), :])
                a = a_refs[i][0, 0, g]
                alo = a * lo if alo is None else alo + a * lo
                ahi = a * hi if ahi is None else ahi + a * hi
            o_ref[tok, 0:P_ROWS, :] += alo
            o_ref[tok, P_ROWS:D_ROWS, :] += ahi
        return carry

    per_chunk = gpc // GROUP_UNROLL
    lax.fori_loop(meta_ref[0, 0, k] * per_chunk, meta_ref[0, 0, k + 1] * per_chunk, groups, 0)


def pack_table(w):
    e = w.shape[0]
    bits = lax.bitcast_convert_type(w.astype(BF16), jnp.uint16).astype(jnp.uint32)
    half = D_MODEL // 2
    packed = (bits[:, half:] << 16) | bits[:, :half]
    return lax.bitcast_convert_type(packed, jnp.int32).reshape(e * P_ROWS, LANES)


def by_lane(a):
    nb, cap = a.shape[0], a.shape[-1]
    a = a.reshape(nb, cap // SUBLANES, SUBLANES)
    return [a[:, :, i].reshape(nb, 1, cap // SUBLANES) for i in range(SUBLANES)]


def peer_route(eidx, gate, *, n_exp, tile, blk, chunk):
    T, S = eidx.shape
    nb = T // blk
    nt = n_exp // tile
    shift = tile.bit_length() - 1
    assert 1 << shift == tile and T % blk == 0 and S + SUBLANES <= 256 and chunk <= 256
    assert blk & (blk - 1) == 0 and blk < 256 and shift <= ROUTE_POS_SHIFT and nt <= 4
    i32 = jnp.int32
    tiles = jnp.arange(nt, dtype=i32)
    tile_id = eidx >> shift
    cnt = jnp.sum(tile_id[:, :, None] == tiles[None, None, :], axis=1, dtype=i32)
    tpad = (-cnt) % SUBLANES
    t_in = (jnp.arange(T, dtype=i32) % blk)[:, None]
    dead = i32(nt << ROUTE_TILE_SHIFT)
    tile_f, tok_f, pos_f = ROUTE_TILE_SHIFT, ROUTE_TOK_SHIFT, ROUTE_POS_SHIFT
    key_real = ((tile_id << tile_f) | (t_in << tok_f) | (jnp.arange(S, dtype=i32)[None, :] << pos_f)
                | (eidx & (tile - 1)))
    i8 = jnp.arange(SUBLANES, dtype=i32)
    key_tpad = jnp.where(i8[None, None, :] < tpad[:, :, None],
                         (tiles[None, :, None] << tile_f) | (t_in[:, :, None] << tok_f)
                         | ((S + i8)[None, None, :] << pos_f), dead)
    n_run = jnp.sum((cnt + tpad).reshape(nb, blk, nt), axis=1)
    bpad = (-n_run) % chunk
    ib = jnp.arange(chunk - SUBLANES, dtype=i32)
    key_bpad = jnp.where(ib[None, None, :] < bpad[:, :, None],
                         (tiles[None, :, None] << tile_f) | i32(blk << tok_f) | (ib[None, None, :] << pos_f), dead)
    n_real, n_tp, n_bp = blk * S, blk * nt * SUBLANES, nt * (chunk - SUBLANES)
    cap = -(-(n_real + n_tp + nt * chunk) // 1024) * 1024
    fill = cap - (n_real + n_tp + n_bp)
    keys = jnp.concatenate([key_real.reshape(nb, n_real), key_tpad.reshape(nb, n_tp), key_bpad.reshape(nb, n_bp),
                            jnp.full((nb, fill), dead, i32)], axis=1)
    gates = jnp.concatenate([gate.reshape(nb, n_real), jnp.zeros((nb, cap - n_real), gate.dtype)], axis=1)
    keys, gates = lax.sort((keys, gates), dimension=1, num_keys=1)
    rows = (keys & ((1 << pos_f) - 1)) * SUBLANES
    gtok = (keys[:, ::SUBLANES] >> tok_f) & (blk - 1)
    n_chunks = (n_run + bpad) // chunk
    cstart = jnp.concatenate([jnp.zeros((nb, 1), i32), jnp.cumsum(n_chunks, axis=1)], axis=1)
    meta = jnp.pad(cstart, ((0, 0), (0, LANES - nt - 1)))
    r3 = lambda a: a.reshape(nb, 1, a.shape[1])
    return r3(meta), by_lane(rows), r3(gtok), r3(gates), cap


def peer_experts(h, res, eidx, gate, u_packed, v_packed, *, tile, blk):
    T = h.shape[0]
    chunk = PEER_CHUNK_SLOTS
    n_exp = u_packed.shape[0] // P_ROWS
    nt = n_exp // tile
    nb = T // blk
    meta, rows, gtok, gates, cap = peer_route(eidx, gate, n_exp=n_exp, tile=tile, blk=blk, chunk=chunk)
    tok_spec = pl.BlockSpec((blk, D_ROWS, LANES), lambda j, k: (j, 0, 0))
    tab_spec = pl.BlockSpec((tile * P_ROWS, LANES), lambda j, k: (k, 0))
    smem = lambda n: pl.BlockSpec((1, 1, n), lambda j, k: (j, 0, 0), memory_space=pltpu.SMEM,
                                  pipeline_mode=pl.Buffered(1))
    slot_spec = pl.BlockSpec((1, 1, cap), lambda j, k: (j, 0, 0))
    params = pltpu.CompilerParams(dimension_semantics=("parallel", "arbitrary"), vmem_limit_bytes=PEER_VMEM_LIMIT)
    a = pl.pallas_call(
        functools.partial(_peer_dot_kernel, chunk=chunk),
        grid=(nb, nt),
        in_specs=[smem(LANES)] + [smem(cap // SUBLANES)] * (SUBLANES + 1) + [tok_spec, slot_spec, tab_spec],
        out_specs=slot_spec,
        out_shape=jax.ShapeDtypeStruct((nb, 1, cap), F32),
        scratch_shapes=[pltpu.VMEM((2, chunk, LANES), F32)],
        compiler_params=params, name="peer_dot",
    )(meta, *rows, gtok, h.reshape(T, D_ROWS, LANES), gates, u_packed)
    out = pl.pallas_call(
        functools.partial(_peer_acc_kernel, chunk=chunk),
        grid=(nb, nt),
        in_specs=[smem(LANES)] + [smem(cap // SUBLANES)] * (2 * SUBLANES + 1) + [tok_spec, tab_spec],
        out_specs=tok_spec,
        out_shape=jax.ShapeDtypeStruct((T, D_ROWS, LANES), F32),
        compiler_params=params, name="peer_acc",
    )(meta, *rows, *by_lane(a.reshape(nb, cap)), gtok, res.reshape(T, D_ROWS, LANES), v_packed)
    return out.reshape(T, D_MODEL)


ATT_W = 3 * N_DIL * D_ATT
ROPE_HALF = ROPE_DIM // 2
ATT_TQ = 128
N_QK_BLOCKS = 2 * N_DIL


def rope_lane_tables(length):
    inv = ROPE_THETA ** (-jnp.arange(0, ROPE_DIM, 2, dtype=F32) / ROPE_DIM)
    ang = jnp.arange(length, dtype=F32)[:, None] * inv[None, :]
    cos, sin = jnp.cos(ang), jnp.sin(ang)
    lane = np.arange(LANES) % HEAD_DIM
    j = lane % ROPE_HALF
    first, second = lane < ROPE_HALF, (lane >= ROPE_HALF) & (lane < ROPE_DIM)
    c = jnp.where((first | second)[None, :], cos[:, j], 1.0)
    s1 = jnp.where(second[None, :], sin[:, j], 0.0)
    s2 = jnp.where(first[None, :], -sin[:, j], 0.0)
    return c, s1, s2


def _rope_cast_kernel(x_ref, c_ref, s1_ref, s2_ref, o_ref, *, q_scale):
    col = pl.program_id(1)

    @pl.when(col < N_QK_BLOCKS)
    def _():
        scale = jnp.where(col < N_DIL, q_scale, 1.0)
        c, s1, s2 = c_ref[...], s1_ref[...], s2_ref[...]
        for t in range(D_ATT // LANES):
            x = x_ref[:, t * LANES:(t + 1) * LANES]
            y = x * c + pltpu.roll(x, ROPE_HALF, axis=1) * s1 + pltpu.roll(x, LANES - ROPE_HALF, axis=1) * s2
            o_ref[:, t * LANES:(t + 1) * LANES] = (y * scale).astype(BF16)

    @pl.when(col >= N_QK_BLOCKS)
    def _():
        o_ref[...] = x_ref[...].astype(BF16)


def rope_cast(att, seq_len, *, tm=512):
    T = att.shape[0]
    assert seq_len % tm == 0
    c, s1, s2 = rope_lane_tables(seq_len)
    nblk = seq_len // tm
    tab = pl.BlockSpec((tm, LANES), lambda i, j: (i % nblk, 0))
    blk = pl.BlockSpec((tm, D_ATT), lambda i, j: (i, j))
    return pl.pallas_call(
        functools.partial(_rope_cast_kernel, q_scale=HEAD_DIM ** -0.5),
        grid=(T // tm, ATT_W // D_ATT),
        in_specs=[blk, tab, tab, tab],
        out_specs=blk,
        out_shape=jax.ShapeDtypeStruct((T, ATT_W), BF16),
        compiler_params=pltpu.CompilerParams(dimension_semantics=("parallel", "parallel")),
        name="rope_cast",
    )(att, c, s1, s2)


def _band_attn_kernel(q_ref, kp_ref, kc_ref, kn_ref, vp_ref, vc_ref, vn_ref, o_ref, lse_ref, s_scr, p_scr,
                      *, hw, sub_len):
    i = pl.program_id(2)
    tq = q_ref.shape[0]
    n_heads = D_ATT // HEAD_DIM
    q_pos = i * tq + lax.broadcasted_iota(jnp.int32, (tq, 3 * tq), 0)
    k_pos = (i - 1) * tq + lax.broadcasted_iota(jnp.int32, (tq, 3 * tq), 1)
    valid = (jnp.abs(k_pos - q_pos) <= hw) & (k_pos >= 0) & (k_pos < sub_len)
    low = lax.broadcasted_iota(jnp.int32, (tq, LANES), 1) < HEAD_DIM
    contract_last = (((1,), (1,)), ((), ()))
    k_refs = (kp_ref, kc_ref, kn_ref)
    v_refs = (vp_ref, vc_ref, vn_ref)
    for h in range(n_heads):
        sl = slice(h // 2 * LANES, (h // 2 + 1) * LANES)
        q2 = q_ref[:, sl]
        qh = jnp.where(low if h % 2 == 0 else ~low, q2, jnp.zeros_like(q2))
        for j in range(3):
            s_scr[h, :, j * tq:(j + 1) * tq] = lax.dot_general(qh, k_refs[j][:, sl], contract_last,
                                                                 preferred_element_type=F32)
    s = jnp.where(valid[None], s_scr[...], NEG)
    m = jnp.max(s, axis=-1, keepdims=True)
    p = jnp.exp(s - m)
    l = jnp.sum(p, axis=-1, keepdims=True)
    p_scr[...] = p.astype(BF16)
    lse = m + jnp.log(l)
    inv = 1.0 / l
    for pair in range(n_heads // 2):
        sl = slice(pair * LANES, (pair + 1) * LANES)
        os = []
        for h in (2 * pair, 2 * pair + 1):
            o = sum(jnp.dot(p_scr[h, :, j * tq:(j + 1) * tq], v_refs[j][:, sl], preferred_element_type=F32)
                    for j in range(3))
            os.append(o * inv[h])
        o_ref[:, sl] = jnp.where(low, os[0], os[1])
        lse_ref[:, sl] = jnp.where(low, lse[2 * pair], lse[2 * pair + 1])


def band_attention(qkv, g, batch, seq_len, *, tq=ATT_TQ):
    win, dil = DILATION_CFG[g]
    hw = win // (2 * dil)
    sub_len = seq_len // dil
    assert sub_len % tq == 0 and hw <= tq
    nq = sub_len // tq
    ncol = ATT_W // D_ATT
    x = qkv.reshape(batch, sub_len, dil * ATT_W)

    def spec(kind, shift):
        def index(b, r, i):
            return (b, jnp.clip(i + shift, 0, nq - 1), r * ncol + kind * N_DIL + g)
        return pl.BlockSpec((None, tq, D_ATT), index)

    out_spec = pl.BlockSpec((None, tq, D_ATT), lambda b, r, i: (b, i, r))
    o, lse = pl.pallas_call(
        functools.partial(_band_attn_kernel, hw=hw, sub_len=sub_len),
        grid=(batch, dil, nq),
        in_specs=[spec(0, 0), spec(1, -1), spec(1, 0), spec(1, 1), spec(2, -1), spec(2, 0), spec(2, 1)],
        out_specs=[out_spec, out_spec],
        out_shape=[jax.ShapeDtypeStruct((batch, sub_len, dil * D_ATT), F32)] * 2,
        scratch_shapes=[pltpu.VMEM((D_ATT // HEAD_DIM, tq, 3 * tq), F32),
                        pltpu.VMEM((D_ATT // HEAD_DIM, tq, 3 * tq), BF16)],
        compiler_params=pltpu.CompilerParams(dimension_semantics=("parallel", "parallel", "parallel"),
                                             vmem_limit_bytes=VMEM_LIMIT_BYTES),
        name=f"band_attn_d{dil}",
    )(x, x, x, x, x, x, x)
    return o.reshape(batch * seq_len, D_ATT), lse.reshape(batch * seq_len, D_ATT)


def _attn_merge_kernel(o0, o1, o2, l0, l1, l2, g_ref, y_ref):
    ls = [l0[...], l1[...], l2[...]]
    m = jnp.maximum(jnp.maximum(ls[0], ls[1]), ls[2])
    es = [jnp.exp(l - m) for l in ls]
    den = es[0] + es[1] + es[2]
    o = (es[0] * o0[...] + es[1] * o1[...] + es[2] * o2[...]) / den
    ms = jnp.mean(o * o, axis=-1, keepdims=True)
    y_ref[...] = o * lax.rsqrt(ms + EPS) * g_ref[...]


def attn_merge(outs, lses, norm_w, *, tm=512):
    T = outs[0].shape[0]
    blk = pl.BlockSpec((tm, D_ATT), lambda i: (i, 0))
    return pl.pallas_call(
        _attn_merge_kernel,
        grid=(T // tm,),
        in_specs=[blk] * 6 + [pl.BlockSpec((1, D_ATT), lambda i: (0, 0))],
        out_specs=blk,
        out_shape=jax.ShapeDtypeStruct((T, D_ATT), F32),
        compiler_params=pltpu.CompilerParams(dimension_semantics=("parallel",)),
        name="attn_merge",
    )(*outs, *lses, norm_w.reshape(1, D_ATT))


def attention_mixer(att, norm_w, batch, seq_len):
    qkv = rope_cast(att, seq_len)
    res = [band_attention(qkv, g, batch, seq_len) for g in range(N_DIL)]
    return attn_merge([r[0] for r in res], [r[1] for r in res], norm_w)


CONV_HALO = SUBLANES


def _conv_taps(x, w_ref, pad_scr, width):
    n = x.shape[0]
    half = width // 2
    zeros = jnp.zeros((CONV_HALO, LANES), F32)
    pad_scr[0:CONV_HALO, :] = zeros
    pad_scr[CONV_HALO + n:CONV_HALO + n + CONV_HALO, :] = zeros
    pad_scr[CONV_HALO:CONV_HALO + n, :] = x
    acc = jnp.zeros((n, LANES), F32)
    for k in range(width):
        acc = acc + pad_scr[CONV_HALO - half + k:CONV_HALO - half + k + n, :] * w_ref[k:k + 1, :]
    return acc


def _silu(x):
    return x * (1.0 / (1.0 + jnp.exp(-x)))


def _ssd_conv_kernel(x_ref, w_ref, b_ref, o_ref, pad_scr):
    o_ref[...] = _silu(_conv_taps(x_ref[...], w_ref, pad_scr, SSD_CONV) + b_ref[...])


def _shortconv_kernel(bg_ref, cg_ref, hx_ref, w_ref, o_ref, pad_scr):
    o_ref[...] = bg_ref[...] * _conv_taps(cg_ref[...] * hx_ref[...], w_ref, pad_scr, SC_CONV)


def _strip_call(kernel, name, n_out, batch, seq_len, in_specs, args):
    return pl.pallas_call(
        kernel,
        grid=(batch, n_out // LANES),
        in_specs=in_specs,
        out_specs=pl.BlockSpec((seq_len, LANES), lambda b, c: (b, c)),
        out_shape=jax.ShapeDtypeStruct((batch * seq_len, n_out), F32),
        scratch_shapes=[pltpu.VMEM((seq_len + 2 * CONV_HALO, LANES), F32)],
        compiler_params=pltpu.CompilerParams(dimension_semantics=("parallel", "parallel"),
                                             vmem_limit_bytes=VMEM_LIMIT_BYTES),
        name=name,
    )(*args)


def ssd_conv(xbc, w, bias, batch, seq_len):
    strip = lambda off: pl.BlockSpec((seq_len, LANES), lambda b, c: (b, c + off))
    taps = pl.BlockSpec((SSD_CONV, LANES), lambda b, c: (0, c))
    vec = pl.BlockSpec((1, LANES), lambda b, c: (0, c))
    return _strip_call(_ssd_conv_kernel, "ssd_conv", XBC, batch, seq_len, [strip(0), taps, vec],
                       (xbc, w, bias.reshape(1, XBC)))


def shortconv(sc, w, batch, seq_len):
    nblk = D_SC // LANES
    strip = lambda off: pl.BlockSpec((seq_len, LANES), lambda b, c: (b, c + off))
    taps = pl.BlockSpec((SC_CONV, LANES), lambda b, c: (0, c))
    return _strip_call(_shortconv_kernel, "shortconv", D_SC, batch, seq_len,
                       [strip(0), strip(nblk), strip(2 * nblk), taps], (sc, sc, sc, w))


HEAD_PAIRS = D_SSD // LANES
PAIRS_PER_GROUP = HEAD_PAIRS // SSD_GROUPS


def _dot3(x, w):
    return sum(jnp.dot(p, w, preferred_element_type=F32) for p in _split3(x))


def _softplus(x):
    return jnp.maximum(x, 0.0) + jnp.log(1.0 + jnp.exp(-jnp.abs(x)))


def _ssd_scan_kernel(xbc_ref, dt_ref, dtt_ref, bias_ref, a_ref, biast_ref, at_ref, erep_ref, eexp_ref,
                     y_ref, h_scr, cumrep_scr):
    d = pl.program_id(1)
    c = pl.program_id(2)
    q = SSD_CHUNK

    @pl.when(c == 0)
    def _():
        h_scr[...] = jnp.zeros_like(h_scr)

    row = lax.broadcasted_iota(jnp.int32, (q, q), 0)
    col = lax.broadcasted_iota(jnp.int32, (q, q), 1)
    sign = jnp.where(d == 0, 1, -1)
    mask = (row - col) * sign >= 0
    mask01 = jnp.where(mask, 1.0, 0.0).astype(BF16)
    low = lax.broadcasted_iota(jnp.int32, (q, LANES), 1) < HEAD_DIM
    contract_last = (((1,), (1,)), ((), ()))
    contract_first = (((0,), (0,)), ((), ()))

    dt = _softplus(dt_ref[...] + bias_ref[...])
    a = dt * a_ref[...]
    cum = sum(jnp.dot(mask01, p, preferred_element_type=F32) for p in _split3(a))
    eexp = eexp_ref[0]
    cumrep_scr[...] = _dot3(cum, erep_ref[0])
    cum_exp = _dot3(cum, eexp)
    dt_exp = _dot3(dt, eexp)
    total_exp = jnp.sum(_dot3(a, eexp), axis=0, keepdims=True)

    heads = pl.ds(pl.multiple_of(d * SSD_HEADS, SUBLANES), SSD_HEADS)
    at = _softplus(dtt_ref[heads, :] + biast_ref[heads, :]) * at_ref[heads, :]
    cum_t = sum(lax.dot_general(p, mask01, contract_last, preferred_element_type=F32) for p in _split3(at))

    for g in range(SSD_GROUPS):
        b_g = xbc_ref[:, D_SSD + g * SSD_STATE:D_SSD + (g + 1) * SSD_STATE].astype(BF16)
        c_g = xbc_ref[:, D_SSD + (SSD_GROUPS + g) * SSD_STATE:D_SSD + (SSD_GROUPS + g + 1) * SSD_STATE].astype(BF16)
        cb = lax.dot_general(c_g, b_g, contract_last, preferred_element_type=F32)
        for pp in range(PAIRS_PER_GROUP):
            p = g * PAIRS_PER_GROUP + pp
            sl = slice(p * LANES, (p + 1) * LANES)
            xdt = xbc_ref[:, sl] * dt_exp[:, sl]
            xdt_b = xdt.astype(BF16)
            yd = []
            for hh in range(2):
                h = 2 * p + hh
                seg = cumrep_scr[:, h * LANES:(h + 1) * LANES] - cum_t[h:h + 1, :]
                m = (cb * jnp.exp(jnp.where(mask, seg, NEG_INF))).astype(BF16)
                yd.append(jnp.dot(m, xdt_b, preferred_element_type=F32))
            cum_p = cum_exp[:, sl]
            h_in = h_scr[p]
            y_off = jnp.dot(c_g, h_in.astype(BF16), preferred_element_type=F32) * jnp.exp(cum_p)
            y_ref[0, :, sl] = jnp.where(low, yd[0], yd[1]) + y_off
            tot_p = total_exp[:, sl]
            xs_end = (xdt * jnp.exp(tot_p - cum_p)).astype(BF16)
            states = lax.dot_general(b_g, xs_end, contract_first, preferred_element_type=F32)
            h_scr[p] = jnp.exp(tot_p) * h_in + states


def head_expanders():
    erep = np.zeros((2, LANES, SSD_HEADS * LANES), np.float32)
    eexp = np.zeros((2, LANES, D_SSD), np.float32)
    for d in range(2):
        for h in range(SSD_HEADS):
            erep[d, d * SSD_HEADS + h, h * LANES:(h + 1) * LANES] = 1.0
            eexp[d, d * SSD_HEADS + h, h * HEAD_DIM:(h + 1) * HEAD_DIM] = 1.0
    return jnp.asarray(erep, BF16), jnp.asarray(eexp, BF16)


def ssd_scan(xbc_c, dt_raw, dt_bias, a_log, batch, seq_len):
    T = xbc_c.shape[0]
    assert seq_len % SSD_CHUNK == 0
    nc = seq_len // SSD_CHUNK
    lane_pad = LANES - 2 * SSD_HEADS
    dt_l = jnp.pad(dt_raw, ((0, 0), (0, lane_pad)))
    dt_t = dt_raw.T
    bias = dt_bias.reshape(2 * SSD_HEADS).astype(F32)
    a_neg = -jnp.exp(a_log.reshape(2 * SSD_HEADS).astype(F32))
    erep, eexp = head_expanders()

    def chunk(b, d, c):
        return b * nc + jnp.where(d == 0, c, nc - 1 - c)

    vec = pl.BlockSpec((1, LANES), lambda b, d, c: (0, 0))
    colv = pl.BlockSpec((2 * SSD_HEADS, 1), lambda b, d, c: (0, 0))
    return pl.pallas_call(
        _ssd_scan_kernel,
        grid=(batch, 2, nc),
        in_specs=[pl.BlockSpec((SSD_CHUNK, XBC), lambda b, d, c: (chunk(b, d, c), 0)),
                  pl.BlockSpec((SSD_CHUNK, LANES), lambda b, d, c: (chunk(b, d, c), 0)),
                  pl.BlockSpec((2 * SSD_HEADS, SSD_CHUNK), lambda b, d, c: (0, chunk(b, d, c))),
                  vec, vec, colv, colv,
                  pl.BlockSpec((1, LANES, SSD_HEADS * LANES), lambda b, d, c: (d, 0, 0)),
                  pl.BlockSpec((1, LANES, D_SSD), lambda b, d, c: (d, 0, 0))],
        out_specs=pl.BlockSpec((1, SSD_CHUNK, D_SSD), lambda b, d, c: (d, chunk(b, d, c), 0)),
        out_shape=jax.ShapeDtypeStruct((2, T, D_SSD), F32),
        scratch_shapes=[pltpu.VMEM((HEAD_PAIRS, SSD_STATE, LANES), F32),
                        pltpu.VMEM((SSD_CHUNK, SSD_HEADS * LANES), F32)],
        compiler_params=pltpu.CompilerParams(dimension_semantics=("parallel", "arbitrary", "arbitrary"),
                                             vmem_limit_bytes=VMEM_LIMIT_BYTES),
        name="ssd_scan",
    )(xbc_c, dt_l, dt_t, jnp.pad(bias, (0, lane_pad)).reshape(1, LANES),
      jnp.pad(a_neg, (0, lane_pad)).reshape(1, LANES), bias.reshape(2 * SSD_HEADS, 1),
      a_neg.reshape(2 * SSD_HEADS, 1), erep, eexp)


def _ssd_gate_norm_kernel(y_ref, xs_ref, z_ref, dskip_ref, nw_ref, o_ref):
    y = (y_ref[0] + y_ref[1] + xs_ref[...] * dskip_ref[...]) * _silu(z_ref[...])
    gw = D_SSD // SSD_GROUPS
    for g in range(SSD_GROUPS):
        yg = y[:, g * gw:(g + 1) * gw]
        ms = jnp.mean(yg * yg, axis=-1, keepdims=True)
        o_ref[:, g * gw:(g + 1) * gw] = yg * lax.rsqrt(ms + EPS) * nw_ref[:, g * gw:(g + 1) * gw]


def ssd_gate_norm(y2, xbc_c, z, d_skip, norm_w, *, tm=512):
    T = z.shape[0]
    dvec = jnp.repeat(d_skip.astype(F32), HEAD_DIM).reshape(1, D_SSD)
    blk = pl.BlockSpec((tm, D_SSD), lambda i: (i, 0))
    vec = pl.BlockSpec((1, D_SSD), lambda i: (0, 0))
    return pl.pallas_call(
        _ssd_gate_norm_kernel,
        grid=(T // tm,),
        in_specs=[pl.BlockSpec((2, tm, D_SSD), lambda i: (0, i, 0)), blk, blk, vec, vec],
        out_specs=blk,
        out_shape=jax.ShapeDtypeStruct((T, D_SSD), F32),
        compiler_params=pltpu.CompilerParams(dimension_semantics=("parallel",), vmem_limit_bytes=VMEM_LIMIT_BYTES),
        name="ssd_gate_norm",
    )(y2, xbc_c, z, dvec, norm_w.reshape(1, D_SSD))


def ssd_mixer(z, xbc, dt_raw, conv_w, conv_b, dt_bias, a_log, d_skip, norm_w, batch, seq_len):
    xbc_c = ssd_conv(xbc, conv_w, conv_b, batch, seq_len)
    y2 = ssd_scan(xbc_c, dt_raw, dt_bias, a_log, batch, seq_len)
    return ssd_gate_norm(y2, xbc_c, z, d_skip, norm_w)


def trunk(x, norm_mix, w_in, ssd_conv_w, ssd_conv_b, ssd_dt_bias, ssd_a_log, ssd_d, ssd_norm,
          sc_conv_w, sc_norm, att_norm, w_out, norm_ffn, peer_wq, peer_subkeys, peer_u, peer_v, norm_final):
    b, L, D = x.shape
    T = b * L
    x = x.reshape(T, D)
    o_xbc = D_SSD
    o_dt = o_xbc + XBC
    o_sc = o_dt + 2 * SSD_HEADS
    o_att = o_sc + 3 * D_SC
    for l in range(DEPTH):
        w = w_in[l]
        g = norm_mix[l]
        w_dt = jnp.pad(w[:, o_dt:o_sc], ((0, 0), (0, DT_PAD - 2 * SSD_HEADS)))
        z = norm_matmul(x, g, w[:, :o_xbc].astype(BF16), tn=768)
        xbc = norm_matmul(x, g, w[:, o_xbc:o_dt].astype(BF16), tn=640)
        dt_raw = norm_matmul(x, g, w_dt.astype(BF16))[:, :2 * SSD_HEADS]
        sc = norm_matmul(x, g, w[:, o_sc:o_att].astype(BF16), tn=768)
        att = norm_matmul(x, g, w[:, o_att:].astype(BF16), tn=768)
        ys = [ssd_mixer(z, xbc, dt_raw, ssd_conv_w[l], ssd_conv_b[l], ssd_dt_bias[l], ssd_a_log[l], ssd_d[l],
                        ssd_norm[l], b, L),
              rmsnorm_rows(shortconv(sc, sc_conv_w[l], b, L), sc_norm[l]),
              attention_mixer(att, att_norm[l], b, L)]
        x = matmul_residual(ys, w_out[l].astype(BF16), x)
        q, h = norm_matmul_emit(x, norm_ffn[l], peer_wq[l].astype(BF16))
        eidx, gate = peer_topk(q, peer_subkeys[l])
        x = peer_experts(h, x, eidx, gate, pack_table(peer_u[l]), pack_table(peer_v[l]),
                         tile=PEER_TILE, blk=PEER_BLK)
    return rmsnorm_rows(x, norm_final).reshape(b, L, D)


def kernel(x_prompt, x_sample, norm_mix, w_in, ssd_conv_w, ssd_conv_b, ssd_dt_bias, ssd_a_log, ssd_d, ssd_norm,
           sc_conv_w, sc_norm, att_norm, w_out, norm_ffn, peer_wq, peer_subkeys, peer_u, peer_v, norm_final):
    nb = x_prompt.shape[0]
    x = jnp.concatenate([x_prompt, x_sample], axis=0)
    y = trunk(x, norm_mix, w_in, ssd_conv_w, ssd_conv_b, ssd_dt_bias, ssd_a_log, ssd_d, ssd_norm,
              sc_conv_w, sc_norm, att_norm, w_out, norm_ffn, peer_wq, peer_subkeys, peer_u, peer_v, norm_final)
    return (y[:nb], y[nb:])
```
